```python
import jax, jax.numpy as jnp
from jax import lax
import numpy as np

D_MODEL = 1024
BATCH = 4
SEQ = 8192
DEPTH = 2

MIX_WIDTH = D_MODEL
POOL_WINDOWS = (2, 4, 8, 16)
POOL_GROUPS = len(POOL_WINDOWS)
POOL_WIDTH = D_MODEL // 4
POOL_GROUP_DIM = POOL_WIDTH // POOL_GROUPS
SGU_CHUNK = 128
SGU_HEADS = 4
SGU_WIDTH = D_MODEL // 2
SGU_HEAD_DIM = SGU_WIDTH // SGU_HEADS
FNET_HEADS = 4
FNET_WIDTH = D_MODEL // 4
FNET_HEAD_DIM = FNET_WIDTH // FNET_HEADS
IN_PROJ_WIDTH = POOL_WIDTH + 2 * SGU_WIDTH + FNET_WIDTH
MEM_LEN = 256
XA_HEADS = 4
XA_HEAD_DIM = D_MODEL // XA_HEADS
FFN_HIDDEN = -(-8 * D_MODEL // (3 * 256)) * 256
RMS_EPS = 1e-6
LN_EPS = 1e-5

kernel_name = "hybrid_pool_sgu_fourier_encoder"


def _rmsnorm(x, g):
    xf = x.astype(jnp.float32)
    y = xf * lax.rsqrt(jnp.mean(xf * xf, axis=-1, keepdims=True) + RMS_EPS)
    return (y * g.astype(jnp.float32)).astype(x.dtype)


def _layernorm(x, g):
    xf = x.astype(jnp.float32)
    mu = jnp.mean(xf, axis=-1, keepdims=True)
    xc = xf - mu
    y = xc * lax.rsqrt(jnp.mean(xc * xc, axis=-1, keepdims=True) + LN_EPS)
    return (y * g.astype(jnp.float32)).astype(x.dtype)


def _pool_mixer(xa, w, scale):
    b, s, _ = xa.shape
    xf = xa.astype(jnp.float32)
    csum = jnp.concatenate(
        [jnp.zeros((b, 1, POOL_WIDTH), jnp.float32), lax.cumsum(xf, axis=1)], axis=1)
    t = jnp.arange(s)
    pooled = []
    for g, win in enumerate(POOL_WINDOWS):
        left = win // 2
        right = win - 1 - left
        lo = jnp.clip(t - left, 0, s - 1)
        hi = jnp.clip(t + right, 0, s - 1)
        cg = csum[..., g * POOL_GROUP_DIM:(g + 1) * POOL_GROUP_DIM]
        window_sum = jnp.take(cg, hi + 1, axis=1) - jnp.take(cg, lo, axis=1)
        count = (hi - lo + 1).astype(jnp.float32)[None, :, None]
        pooled.append(window_sum / count)
    pooled = jnp.stack(pooled, axis=2)
    diff = (pooled - xf.reshape(b, s, POOL_GROUPS, POOL_GROUP_DIM)).astype(xa.dtype)
    y = jnp.einsum('bsgc,gcd->bsgd', diff, w).reshape(b, s, POOL_WIDTH)
    return y * scale


def _spatial_gating(u, v, v_gain, w_s, b_s):
    b, s, _ = u.shape
    v = _layernorm(v, v_gain)
    vc = v.reshape(b, s // SGU_CHUNK, SGU_CHUNK, SGU_HEADS, SGU_HEAD_DIM)
    mixed = jnp.einsum('hpq,bnqhc->bnphc', w_s, vc) + jnp.transpose(b_s)[None, None, :, :, None]
    return u * mixed.reshape(b, s, SGU_WIDTH)


def _fourier_mixer(xc, w):
    b, s, _ = xc.shape
    xh = xc.astype(jnp.float32).reshape(b, s, FNET_HEADS, FNET_HEAD_DIM)
    f = jnp.fft.fftn(xh, axes=(1, 3), norm='ortho').real.astype(xc.dtype)
    return jnp.einsum('bshc,hcd->bshd', f, w).reshape(b, s, FNET_WIDTH)


def _memory_attention(h, m, wq, wk, wv, wo):
    b, s, _ = h.shape
    q = (h @ wq).reshape(b, s, XA_HEADS, XA_HEAD_DIM)
    k = (m @ wk).reshape(b, MEM_LEN, XA_HEADS, XA_HEAD_DIM)
    v = (m @ wv).reshape(b, MEM_LEN, XA_HEADS, XA_HEAD_DIM)
    scores = jnp.einsum('bshd,bmhd->bhsm', q, k).astype(jnp.float32) * (XA_HEAD_DIM ** -0.5)
    p = jax.nn.softmax(scores, axis=-1).astype(h.dtype)
    o = jnp.einsum('bhsm,bmhd->bshd', p, v).reshape(b, s, D_MODEL)
    return o @ wo


def _swiglu(h, wg, wu, wd):
    return (jax.nn.silu(h @ wg) * (h @ wu)) @ wd


def setup_inputs(seed: int = 0) -> dict:
    key = jax.random.key(seed)
    ks = jax.random.split(key, 32)
    L, D = DEPTH, D_MODEL

    def nrm(k, shape, fan_in):
        return jax.random.normal(k, shape, jnp.float32) * (fan_in ** -0.5)

    def gain(k, shape):
        return 1.0 + 0.05 * jax.random.normal(k, shape, jnp.float32)

    return {
        "x": jax.random.normal(ks[0], (BATCH, SEQ, D), jnp.float32),
        "mem": jax.random.normal(ks[1], (BATCH, MEM_LEN, D), jnp.float32),
        "ln_mix_pre": gain(ks[2], (L, D)),
        "w_in": nrm(ks[3], (L, D, IN_PROJ_WIDTH), D),
        "pool_w": nrm(ks[4], (L, POOL_GROUPS, POOL_GROUP_DIM, POOL_GROUP_DIM), POOL_GROUP_DIM),
        "pool_scale": gain(ks[5], (L, POOL_WIDTH)),
        "sgu_norm": gain(ks[6], (L, SGU_WIDTH)),
        "sgu_w": nrm(ks[7], (L, SGU_HEADS, SGU_CHUNK, SGU_CHUNK), SGU_CHUNK),
        "sgu_b": gain(ks[8], (L, SGU_HEADS, SGU_CHUNK)),
        "fnet_w": nrm(ks[9], (L, FNET_HEADS, FNET_HEAD_DIM, FNET_HEAD_DIM), FNET_HEAD_DIM),
        "w_out": nrm(ks[10], (L, MIX_WIDTH, D), MIX_WIDTH),
        "ln_mix_post": gain(ks[11], (L, D)),
        "ln_xa_pre": gain(ks[12], (L, D)),
        "ln_mem": gain(ks[13], (L, D)),
        "xa_wq": nrm(ks[14], (L, D, D), D),
        "xa_wk": nrm(ks[15], (L, D, D), D),
        "xa_wv": nrm(ks[16], (L, D, D), D),
        "xa_wo": nrm(ks[17], (L, D, D), D),
        "ln_xa_post": gain(ks[18], (L, D)),
        "ln_ffn_pre": gain(ks[19], (L, D)),
        "ffn_wg": nrm(ks[20], (L, D, FFN_HIDDEN), D),
        "ffn_wu": nrm(ks[21], (L, D, FFN_HIDDEN), D),
        "ffn_wd": nrm(ks[22], (L, FFN_HIDDEN, D), FFN_HIDDEN),
        "ln_ffn_post": gain(ks[23], (L, D)),
    }


def reference(x, mem, ln_mix_pre, w_in, pool_w, pool_scale, sgu_norm, sgu_w, sgu_b,
              fnet_w, w_out, ln_mix_post, ln_xa_pre, ln_mem, xa_wq, xa_wk, xa_wv,
              xa_wo, ln_xa_post, ln_ffn_pre, ffn_wg, ffn_wu, ffn_wd, ln_ffn_post):
    s_a = POOL_WIDTH
    s_u = s_a + SGU_WIDTH
    s_v = s_u + SGU_WIDTH
    for l in range(DEPTH):
        h = _rmsnorm(x, ln_mix_pre[l])
        z = h @ w_in[l]
        za = z[..., :s_a]
        zu = z[..., s_a:s_u]
        zv = z[..., s_u:s_v]
        zc = z[..., s_v:]
        ya = _pool_mixer(za, pool_w[l], pool_scale[l])
        yb = _spatial_gating(zu, zv, sgu_norm[l], sgu_w[l], sgu_b[l])
        yc = _fourier_mixer(zc, fnet_w[l])
        y = jnp.concatenate([ya, yb, yc], axis=-1) @ w_out[l]
        x = x + _rmsnorm(y, ln_mix_post[l])
        m = _rmsnorm(mem, ln_mem[l])
        h = _rmsnorm(x, ln_xa_pre[l])
        y = _memory_attention(h, m, xa_wq[l], xa_wk[l], xa_wv[l], xa_wo[l])
        x = x + _rmsnorm(y, ln_xa_post[l])
        h = _rmsnorm(x, ln_ffn_pre[l])
        y = _swiglu(h, ffn_wg[l], ffn_wu[l], ffn_wd[l])
        x = x + _rmsnorm(y, ln_ffn_post[l])
    return x
```

```python
import functools

import numpy as np
import jax
import jax.numpy as jnp
from jax import lax
from jax.experimental import pallas as pl
from jax.experimental.pallas import tpu as pltpu

F32 = jnp.float32
BF16 = jnp.bfloat16

D = 1024
POOL_W = 256
SGU_W = 512
FNET_W = 256
IN_W = POOL_W + 2 * SGU_W + FNET_W
SGU_CHUNK = 128
SGU_HEADS = 4
MEM_LEN = 256
XA_HEADS = 4
XA_HD = D // XA_HEADS
FFN_H = 2816
RMS_EPS = 1e-6
LN_EPS = 1e-5

FFT_N1 = 64
FFT_N2 = 128
FFT_KG = 8
FFT_A_LANES = 8192
POOL_HALO = 16
TOK_TILE = 512
ATT_TILE = 1024
ATT_SUB = 2
FFN_CHUNKS = (768, 768, 768, 512)
VMEM_LIMIT = 56 * 1024 * 1024


def _rms(xf, g):
    ms = jnp.mean(xf * xf, axis=-1, keepdims=True)
    return xf * lax.rsqrt(ms + RMS_EPS) * g


def _dot(a, b):
    return jnp.dot(a, b, preferred_element_type=F32)


def _params(n_axes):
    return pltpu.CompilerParams(
        dimension_semantics=("arbitrary",) * n_axes, vmem_limit_bytes=VMEM_LIMIT)


def _const_spec(shape):
    nd = len(shape)
    return pl.BlockSpec(shape, lambda *_: (0,) * nd, pipeline_mode=pl.Buffered(1))


def _layer_spec(shape, layer, block=0):
    nd = len(shape)
    return pl.BlockSpec((None,) + tuple(shape), lambda *_: (layer, block) + (0,) * (nd - 1),
                        pipeline_mode=pl.Buffered(1))


@functools.lru_cache(maxsize=None)
def _fft_tables(seq):
    assert seq == FFT_N1 * FFT_N2
    two_pi = 2.0 * np.pi
    k1 = np.arange(FFT_N1)
    ang = two_pi * ((k1[:, None] * k1[None, :]) % FFT_N1) / FFT_N1
    fa = np.concatenate([np.cos(ang), -np.sin(ang)], axis=0) / np.sqrt(FFT_N1)
    k2 = np.arange(FFT_N2)
    k = k1[:, None, None] + FFT_N1 * k2[None, :, None]
    ph = two_pi * ((k * k2[None, None, :]) % seq) / seq
    c, s = np.cos(ph), np.sin(ph)
    fb = np.concatenate([np.concatenate([c, s], axis=2),
                         np.concatenate([-s, c], axis=2)], axis=1) / np.sqrt(FFT_N2)
    hd = FNET_W // 4
    cc = np.arange(hd)
    ang = two_pi * ((cc[:, None] * cc[None, :]) % hd) / hd
    eye = np.eye(4)
    dftc = np.concatenate([np.kron(eye, np.cos(ang)), np.kron(eye, np.sin(ang))], axis=0) / np.sqrt(hd)
    return tuple(np.asarray(t, dtype=np.float32) for t in (fa, fb, dftc))


def _block_diag(w):
    l, g, c, _ = w.shape
    eye = jnp.eye(g, dtype=w.dtype)
    return (eye[None, :, None, :, None] * w[:, :, :, None, :]).reshape(l, g * c, g * c)


def _in_proj_kernel(x_ref, g_ref, w_ref, sg_ref, za_ref, zu_ref, zv_ref, zc_ref):
    h = _rms(x_ref[...], g_ref[...]).astype(BF16)
    z = _dot(h, w_ref[...])
    za_ref[...] = z[:, :POOL_W].astype(BF16)
    zu_ref[...] = z[:, POOL_W:POOL_W + SGU_W].astype(BF16)
    v = z[:, POOL_W + SGU_W:POOL_W + 2 * SGU_W]
    vc = v - jnp.mean(v, axis=-1, keepdims=True)
    vn = vc * lax.rsqrt(jnp.mean(vc * vc, axis=-1, keepdims=True) + LN_EPS) * sg_ref[...]
    zv_ref[...] = vn.astype(BF16)
    zc_ref[...] = z[:, POOL_W + 2 * SGU_W:].astype(BF16)


def _in_proj(x2, g, w, sg, layer):
    n = x2.shape[0]
    t = TOK_TILE
    row = lambda w_: pl.BlockSpec((t, w_), lambda i: (i, 0))
    return pl.pallas_call(
        _in_proj_kernel,
        grid=(n // t,),
        in_specs=[row(D), _layer_spec((1, D), layer), _layer_spec((D, IN_W), layer),
                  _layer_spec((1, SGU_W), layer)],
        out_specs=[row(POOL_W), row(SGU_W), row(SGU_W), row(FNET_W)],
        out_shape=[jax.ShapeDtypeStruct((n, POOL_W), BF16), jax.ShapeDtypeStruct((n, SGU_W), BF16),
                   jax.ShapeDtypeStruct((n, SGU_W), BF16), jax.ShapeDtypeStruct((n, FNET_W), BF16)],
        compiler_params=_params(1),
        name="in_proj",
    )(x2, g, w, sg)


def _fft_a_kernel(fa_ref, x_ref, o_ref):
    o_ref[0] = _dot(fa_ref[...], x_ref[0]).astype(BF16)


def _fft_stage_a(xr, fa):
    b, n1, w = xr.shape
    lb = FFT_A_LANES
    return pl.pallas_call(
        _fft_a_kernel,
        grid=(b, w // lb),
        in_specs=[_const_spec((2 * n1, n1)), pl.BlockSpec((1, n1, lb), lambda i, j: (i, 0, j))],
        out_specs=pl.BlockSpec((1, 2 * n1, lb), lambda i, j: (i, 0, j)),
        out_shape=jax.ShapeDtypeStruct((b, 2 * n1, w), BF16),
        compiler_params=_params(2),
        name="fft_stage_a",
    )(fa, xr)


def _fft_b_kernel(fb_ref, a_ref, fr_ref, fi_ref):
    c = a_ref.shape[-1]
    for j in range(FFT_KG):
        d = jnp.concatenate([a_ref[0, 0, j], a_ref[0, 1, j]], axis=0)
        y = _dot(fb_ref[j], d)
        fr_ref[0, :, j * c:(j + 1) * c] = y[:FFT_N2].astype(BF16)
        fi_ref[0, :, j * c:(j + 1) * c] = y[FFT_N2:].astype(BF16)


def _fft_stage_b(a5, fb):
    b, _, n1, n2, c = a5.shape
    kg = FFT_KG
    return pl.pallas_call(
        _fft_b_kernel,
        grid=(n1 // kg, b),
        in_specs=[pl.BlockSpec((kg, 2 * n2, 2 * n2), lambda g, i: (g, 0, 0)),
                  pl.BlockSpec((1, 2, kg, n2, c), lambda g, i: (i, 0, g, 0, 0))],
        out_specs=[pl.BlockSpec((1, n2, kg * c), lambda g, i: (i, 0, g))] * 2,
        out_shape=[jax.ShapeDtypeStruct((b, n2, n1 * c), BF16)] * 2,
        compiler_params=_params(2),
        name="fft_stage_b",
    )(fb, a5)


def _fold_kernel(dftc_ref, fw_ref, wo_ref, o_ref):
    t = _dot(dftc_ref[...], fw_ref[...]).astype(BF16)
    o_ref[...] = _dot(t, wo_ref[...]).astype(BF16)


def _fold(dftc, fw_bd, w_out):
    depth = fw_bd.shape[0]
    wo_blk = (POOL_W + SGU_W) // FNET_W
    return pl.pallas_call(
        _fold_kernel,
        grid=(depth,),
        in_specs=[_const_spec((2 * FNET_W, FNET_W)),
                  pl.BlockSpec((None, FNET_W, FNET_W), lambda l: (l, 0, 0)),
                  pl.BlockSpec((None, FNET_W, D), lambda l: (l, wo_blk, 0))],
        out_specs=pl.BlockSpec((None, 2 * FNET_W, D), lambda l: (l, 0, 0)),
        out_shape=jax.ShapeDtypeStruct((depth, 2 * FNET_W, D), BF16),
        compiler_params=_params(1),
        name="fold_fourier",
    )(dftc, fw_bd, w_out)


def _mix_out_kernel(x_ref, za_ref, zp_ref, zn_ref, zu_ref, zv_ref, fr_ref, fi_ref,
                    pw_ref, ps_ref, sw_ref, sb_ref, wo_ref, m2_ref, g_ref, o_ref, ext_ref, ym_ref, *, seq):
    t = TOK_TILE
    hl = POOL_HALO
    j = pl.program_id(1)
    nj = seq // t
    cur = za_ref[...].astype(F32)
    ext_ref[0:hl] = jnp.where(j > 0, zp_ref[...].astype(F32), 0.0)
    ext_ref[hl:hl + t] = cur
    ext_ref[hl + t:2 * hl + t] = jnp.where(j < nj - 1, zn_ref[...].astype(F32), 0.0)

    def sh(o, lt):
        return ext_ref[hl + o:hl + o + t, lt * 128:(lt + 1) * 128]

    lane = lax.broadcasted_iota(jnp.int32, (t, 128), 1)
    pos = j * t + lax.broadcasted_iota(jnp.int32, (t, 128), 0)
    low = lane < 64

    def count(l_small, r_small, l_big, r_big):
        left = jnp.where(low, l_small, l_big)
        right = jnp.where(low, r_small, r_big)
        hi = jnp.minimum(pos + right, seq - 1)
        lo = jnp.maximum(pos - left, 0)
        return (hi - lo + 1).astype(F32)

    w2 = sh(-1, 0) + sh(0, 0)
    w4 = w2 + sh(-2, 0) + sh(1, 0)
    w8 = sh(-4, 1)
    for o in range(-3, 4):
        w8 = w8 + sh(o, 1)
    w16 = w8
    for o in (-8, -7, -6, -5, 4, 5, 6, 7):
        w16 = w16 + sh(o, 1)
    p0 = jnp.where(low, w2, w4) / count(1, 0, 2, 1)
    p1 = jnp.where(low, w8, w16) / count(4, 3, 8, 7)
    diff = (jnp.concatenate([p0, p1], axis=1) - cur).astype(BF16)
    ya = _dot(diff, pw_ref[...]) * ps_ref[...]
    ym_ref[:, 0:POOL_W] = ya.astype(BF16)

    nch = t // SGU_CHUNK
    hdim = SGU_W // SGU_HEADS
    for h in range(SGU_HEADS):
        vh = jnp.concatenate(
            [zv_ref[c * SGU_CHUNK:(c + 1) * SGU_CHUNK, h * hdim:(h + 1) * hdim] for c in range(nch)], axis=1)
        mh = _dot(sw_ref[h], vh)
        for c in range(nch):
            mixed = mh[:, c * hdim:(c + 1) * hdim] + sb_ref[h]
            u = zu_ref[c * SGU_CHUNK:(c + 1) * SGU_CHUNK, h * hdim:(h + 1) * hdim].astype(F32)
            ym_ref[c * SGU_CHUNK:(c + 1) * SGU_CHUNK,
                   POOL_W + h * hdim:POOL_W + (h + 1) * hdim] = (u * mixed).astype(BF16)

    y = _dot(ym_ref[...], wo_ref[...])
    y = y + _dot(jnp.concatenate([fr_ref[...], fi_ref[...]], axis=1), m2_ref[...])
    o_ref[...] = x_ref[...] + _rms(y, g_ref[...])


def _mix_out(x2, za, zu, zv, fr, fi, pw_bd, ps, sw, sb_full, w_out, m2, g, layer, *, batch, seq):
    n = x2.shape[0]
    t = TOK_TILE
    hl = POOL_HALO
    nj = seq // t
    kdim = POOL_W + SGU_W
    row = lambda w_: pl.BlockSpec((t, w_), lambda b, j: (b * nj + j, 0))
    prev = pl.BlockSpec((hl, POOL_W), lambda b, j: (jnp.maximum((b * nj + j) * (t // hl) - 1, 0), 0))
    nxt = pl.BlockSpec((hl, POOL_W), lambda b, j: (jnp.minimum((b * nj + j + 1) * (t // hl), n // hl - 1), 0))
    return pl.pallas_call(
        functools.partial(_mix_out_kernel, seq=seq),
        grid=(batch, nj),
        in_specs=[row(D), row(POOL_W), prev, nxt, row(SGU_W), row(SGU_W), row(FNET_W), row(FNET_W),
                  _layer_spec((POOL_W, POOL_W), layer), _layer_spec((1, POOL_W), layer),
                  _layer_spec((SGU_HEADS, SGU_CHUNK, SGU_CHUNK), layer),
                  _layer_spec((SGU_HEADS, SGU_CHUNK, SGU_CHUNK), layer),
                  _layer_spec((kdim, D), layer), _layer_spec((2 * FNET_W, D), layer),
                  _layer_spec((1, D), layer)],
        out_specs=row(D),
        out_shape=jax.ShapeDtypeStruct((n, D), F32),
        scratch_shapes=[pltpu.VMEM((t + 2 * hl, POOL_W), F32), pltpu.VMEM((t, kdim), BF16)],
        compiler_params=_params(2),
        name="mix_out",
    )(x2, za, za, za, zu, zv, fr, fi, pw_bd, ps, sw, sb_full, w_out, m2, g)


def _kv_kernel(m_ref, g_ref, wk_ref, wv_ref, k_ref, v_ref):
    m = _rms(m_ref[0], g_ref[...]).astype(BF16)
    k_ref[...] = _dot(m, wk_ref[...]).astype(BF16)
    v_ref[...] = _dot(m, wv_ref[...]).astype(BF16)


def _kv_proj(mem, g, wk, wv):
    b = mem.shape[0]
    depth = wk.shape[0]
    per_layer = lambda shape: pl.BlockSpec((None,) + shape, lambda l, i: (l,) + (0,) * len(shape))
    out = pl.BlockSpec((None, None, MEM_LEN, D), lambda l, i: (l, i, 0, 0))
    return pl.pallas_call(
        _kv_kernel,
        grid=(depth, b),
        in_specs=[pl.BlockSpec((1, MEM_LEN, D), lambda l, i: (i, 0, 0)),
                  per_layer((1, D)), per_layer((D, D)), per_layer((D, D))],
        out_specs=[out, out],
        out_shape=[jax.ShapeDtypeStruct((depth, b, MEM_LEN, D), BF16)] * 2,
        compiler_params=_params(2),
        name="kv_proj",
    )(mem, g, wk, wv)


def _attn_kernel(x_ref, k_ref, v_ref, gpre_ref, wq_ref, wo_ref, gpost_ref, o_ref, oh_ref):
    rows = x_ref.shape[0] // ATT_SUB
    for sub in range(ATT_SUB):
        r = slice(sub * rows, (sub + 1) * rows)
        x = x_ref[r, :]
        h = _rms(x, gpre_ref[...]).astype(BF16)
        q = (_dot(h, wq_ref[...]) * (XA_HD ** -0.5)).astype(BF16)
        for hd in range(XA_HEADS):
            sl = slice(hd * XA_HD, (hd + 1) * XA_HD)
            s = lax.dot_general(q[:, sl], k_ref[:, sl], (((1,), (1,)), ((), ())),
                                preferred_element_type=F32)
            e = jnp.exp(s - jnp.max(s, axis=-1, keepdims=True))
            p = (e / jnp.sum(e, axis=-1, keepdims=True)).astype(BF16)
            oh_ref[r, sl] = _dot(p, v_ref[:, sl]).astype(BF16)
        y = _dot(oh_ref[r, :], wo_ref[...])
        o_ref[r, :] = x + _rms(y, gpost_ref[...])


def _attention(x2, k, v, gpre, wq, wo, gpost, layer, *, batch, seq):
    n = x2.shape[0]
    t = ATT_TILE
    nj = seq // t
    row = pl.BlockSpec((t, D), lambda b, j: (b * nj + j, 0))
    kv = pl.BlockSpec((None, None, MEM_LEN, D), lambda b, j: (layer, b, 0, 0))
    return pl.pallas_call(
        _attn_kernel,
        grid=(batch, nj),
        in_specs=[row, kv, kv, _layer_spec((1, D), layer), _layer_spec((D, D), layer),
                  _layer_spec((D, D), layer), _layer_spec((1, D), layer)],
        out_specs=row,
        out_shape=jax.ShapeDtypeStruct((n, D), F32),
        scratch_shapes=[pltpu.VMEM((t, D), BF16)],
        compiler_params=_params(2),
        name="mem_attention",
    )(x2, k, v, gpre, wq, wo, gpost)


def _ffn_kernel(x_ref, gpre_ref, wg_ref, wu_ref, wd_ref, gpost_ref, o_ref):
    x = x_ref[...]
    h = _rms(x, gpre_ref[...]).astype(BF16)
    y = None
    off = 0
    for ch in FFN_CHUNKS:
        g = _dot(h, wg_ref[:, off:off + ch])
        u = _dot(h, wu_ref[:, off:off + ch])
        a = (g * jax.nn.sigmoid(g) * u).astype(BF16)
        part = _dot(a, wd_ref[off:off + ch, :])
        y = part if y is None else y + part
        off += ch
    o_ref[...] = x + _rms(y, gpost_ref[...])


def _ffn(x2, gpre, wg, wu, wd, gpost, layer):
    n = x2.shape[0]
    t = TOK_TILE
    row = pl.BlockSpec((t, D), lambda i: (i, 0))
    return pl.pallas_call(
        _ffn_kernel,
        grid=(n // t,),
        in_specs=[row, _layer_spec((1, D), layer), _layer_spec((D, FFN_H), layer),
                  _layer_spec((D, FFN_H), layer), _layer_spec((FFN_H, D), layer),
                  _layer_spec((1, D), layer)],
        out_specs=row,
        out_shape=jax.ShapeDtypeStruct((n, D), F32),
        compiler_params=_params(1),
        name="swiglu",
    )(x2, gpre, wg, wu, wd, gpost)


def kernel(x, mem, ln_mix_pre, w_in, pool_w, pool_scale, sgu_norm, sgu_w, sgu_b, fnet_w, w_out,
           ln_mix_post, ln_xa_pre, ln_mem, xa_wq, xa_wk, xa_wv, xa_wo, ln_xa_post, ln_ffn_pre,
           ffn_wg, ffn_wu, ffn_wd, ln_ffn_post):
    batch, seq, d = x.shape
    depth = w_in.shape[0]
    assert d == D and sum(FFN_CHUNKS) == FFN_H and seq % ATT_TILE == 0 and seq % TOK_TILE == 0
    fa, fb, dftc = (jnp.asarray(t).astype(BF16) for t in _fft_tables(seq))
    n = batch * seq
    x2 = x.reshape(n, d)
    vec = lambda a: a.reshape(depth, 1, -1).astype(F32)
    bf = lambda a: a.astype(BF16)

    g_mix_pre, g_mix_post, g_xa_pre, g_xa_post = vec(ln_mix_pre), vec(ln_mix_post), vec(ln_xa_pre), vec(ln_xa_post)
    g_ffn_pre, g_ffn_post, g_mem = vec(ln_ffn_pre), vec(ln_ffn_post), vec(ln_mem)
    sg, ps = vec(sgu_norm), vec(pool_scale)
    w_in_b, w_out_b = bf(w_in), bf(w_out)
    wq_b, wk_b, wv_b, wo_b = bf(xa_wq), bf(xa_wk), bf(xa_wv), bf(xa_wo)
    wg_b, wu_b, wd_b = bf(ffn_wg), bf(ffn_wu), bf(ffn_wd)
    pw_bd = bf(_block_diag(pool_w))
    sw_b = bf(sgu_w)
    sb_full = jnp.broadcast_to(sgu_b[:, :, :, None], sgu_b.shape + (SGU_CHUNK,)).astype(F32)
    m2 = _fold(dftc, bf(_block_diag(fnet_w)), w_out_b)
    k_all, v_all = _kv_proj(mem, g_mem, wk_b, wv_b)

    for l in range(depth):
        za, zu, zv, zc = _in_proj(x2, g_mix_pre, w_in_b, sg, l)
        a = _fft_stage_a(zc.reshape(batch, FFT_N1, FFT_N2 * FNET_W), fa)
        fr, fi = _fft_stage_b(a.reshape(batch, 2, FFT_N1, FFT_N2, FNET_W), fb)
        x2 = _mix_out(x2, za, zu, zv, fr.reshape(n, FNET_W), fi.reshape(n, FNET_W), pw_bd, ps, sw_b, sb_full,
                      w_out_b, m2, g_mix_post, l, batch=batch, seq=seq)
        x2 = _attention(x2, k_all, v_all, g_xa_pre, wq_b, wo_b, g_xa_post, l, batch=batch, seq=seq)
        x2 = _ffn(x2, g_ffn_pre, wg_b, wu_b, wd_b, g_ffn_post, l)
    return x2.reshape(batch, seq, d)
```

```python
import functools

import numpy as np
import jax
import jax.numpy as jnp
from jax import lax
from jax.experimental import pallas as pl
from jax.experimental.pallas import tpu as pltpu

F32 = jnp.float32
BF16 = jnp.bfloat16

D = 1024
POOL_W = 256
POOL_WINDOWS = (2, 4, 8, 16)
SGU_W = 512
FNET_W = 256
IN_W = POOL_W + 2 * SGU_W + FNET_W
MIX_K = POOL_W + SGU_W + 2 * FNET_W
SGU_CHUNK = 128
SGU_HEADS = 4
MEM_LEN = 256
XA_HEADS = 4
XA_HD = D // XA_HEADS
FFN_H = 2816
RMS_EPS = 1e-6
LN_EPS = 1e-5

FFT_N1 = 64
FFT_N2 = 128
FFT_KG = 8
FFT_A_LANES = 8192
POOL_HALO = 16
IN_TILE, IN_SUB = 2048, 4
MIX_TILE, MIX_SUB = 1024, 2
ATT_TILE, ATT_SUB = 1024, 2
FFN_TILE, FFN_SUB = 1024, 2
FFN_CHUNKS = (768, 768, 768, 512)
VMEM_LIMIT = 56 * 1024 * 1024


def _rms(xf, g):
    ms = jnp.mean(xf * xf, axis=-1, keepdims=True)
    return xf * lax.rsqrt(ms + RMS_EPS) * g


def _dot(a, b):
    return jnp.dot(a, b, preferred_element_type=F32)


def _params(n_axes):
    return pltpu.CompilerParams(
        dimension_semantics=("arbitrary",) * n_axes, vmem_limit_bytes=VMEM_LIMIT)


def _const_spec(shape):
    nd = len(shape)
    return pl.BlockSpec(shape, lambda *_: (0,) * nd, pipeline_mode=pl.Buffered(1))


def _layer_spec(shape, layer):
    nd = len(shape)
    return pl.BlockSpec((None,) + tuple(shape), lambda *_: (layer,) + (0,) * nd,
                        pipeline_mode=pl.Buffered(1))


@functools.lru_cache(maxsize=None)
def _fft_tables(seq):
    assert seq == FFT_N1 * FFT_N2
    two_pi = 2.0 * np.pi
    k1 = np.arange(FFT_N1)
    ang = two_pi * ((k1[:, None] * k1[None, :]) % FFT_N1) / FFT_N1
    fa = np.concatenate([np.cos(ang), -np.sin(ang)], axis=0) / np.sqrt(FFT_N1)
    k2 = np.arange(FFT_N2)
    k = k1[:, None, None] + FFT_N1 * k2[None, :, None]
    ph = two_pi * ((k * k2[None, None, :]) % seq) / seq
    c, s = np.cos(ph), np.sin(ph)
    fb = np.concatenate([np.concatenate([c, s], axis=2),
                         np.concatenate([-s, c], axis=2)], axis=1) / np.sqrt(FFT_N2)
    hd = FNET_W // 4
    cc = np.arange(hd)
    ang = two_pi * ((cc[:, None] * cc[None, :]) % hd) / hd
    eye = np.eye(4)
    dftc = np.concatenate([np.kron(eye, np.cos(ang)), np.kron(eye, np.sin(ang))], axis=0) / np.sqrt(hd)
    return tuple(np.asarray(t, dtype=np.float32) for t in (fa, fb, dftc))


@functools.lru_cache(maxsize=None)
def _pool_edge_scales(seq):
    hl = POOL_HALO
    pos = np.concatenate([np.arange(hl), np.arange(seq - hl, seq)])[:, None]
    win = np.repeat(np.asarray(POOL_WINDOWS), POOL_W // len(POOL_WINDOWS))[None, :]
    left = win // 2
    right = win - 1 - left
    cnt = np.minimum(pos + right, seq - 1) - np.maximum(pos - left, 0) + 1
    return (1.0 / cnt).astype(np.float32).reshape(2, hl, POOL_W)


def _block_diag(w):
    l, g, c, _ = w.shape
    eye = jnp.eye(g, dtype=w.dtype)
    return (eye[None, :, None, :, None] * w[:, :, :, None, :]).reshape(l, g * c, g * c)


def _in_proj_kernel(x_ref, g_ref, w_ref, sg_ref, za_ref, zu_ref, zv_ref, zc_ref):
    rows = x_ref.shape[0] // IN_SUB
    for sub in range(IN_SUB):
        r = slice(sub * rows, (sub + 1) * rows)
        h = _rms(x_ref[r, :], g_ref[...]).astype(BF16)
        z = _dot(h, w_ref[...])
        za_ref[r, :] = z[:, :POOL_W].astype(BF16)
        zu_ref[r, :] = z[:, POOL_W:POOL_W + SGU_W].astype(BF16)
        v = z[:, POOL_W + SGU_W:POOL_W + 2 * SGU_W]
        vc = v - jnp.mean(v, axis=-1, keepdims=True)
        vn = vc * lax.rsqrt(jnp.mean(vc * vc, axis=-1, keepdims=True) + LN_EPS) * sg_ref[...]
        zv_ref[r, :] = vn.astype(BF16)
        zc_ref[r, :] = z[:, POOL_W + 2 * SGU_W:].astype(BF16)


def _in_proj(x2, g, w, sg, layer):
    n = x2.shape[0]
    t = IN_TILE
    row = lambda w_: pl.BlockSpec((t, w_), lambda i: (i, 0))
    return pl.pallas_call(
        _in_proj_kernel,
        grid=(n // t,),
        in_specs=[row(D), _layer_spec((1, D), layer), _layer_spec((D, IN_W), layer),
                  _layer_spec((1, SGU_W), layer)],
        out_specs=[row(POOL_W), row(SGU_W), row(SGU_W), row(FNET_W)],
        out_shape=[jax.ShapeDtypeStruct((n, POOL_W), BF16), jax.ShapeDtypeStruct((n, SGU_W), BF16),
                   jax.ShapeDtypeStruct((n, SGU_W), BF16), jax.ShapeDtypeStruct((n, FNET_W), BF16)],
        compiler_params=_params(1),
        name="in_proj",
    )(x2, g, w, sg)


def _fft_a_kernel(fa_ref, x_ref, o_ref):
    o_ref[0] = _dot(fa_ref[...], x_ref[0]).astype(BF16)


def _fft_stage_a(xr, fa):
    b, n1, w = xr.shape
    lb = FFT_A_LANES
    return pl.pallas_call(
        _fft_a_kernel,
        grid=(b, w // lb),
        in_specs=[_const_spec((2 * n1, n1)), pl.BlockSpec((1, n1, lb), lambda i, j: (i, 0, j))],
        out_specs=pl.BlockSpec((1, 2 * n1, lb), lambda i, j: (i, 0, j)),
        out_shape=jax.ShapeDtypeStruct((b, 2 * n1, w), BF16),
        compiler_params=_params(2),
        name="fft_stage_a",
    )(fa, xr)


def _fft_b_kernel(fb_ref, a_ref, fr_ref, fi_ref):
    c = a_ref.shape[-1]
    for j in range(FFT_KG):
        d = jnp.concatenate([a_ref[0, 0, j], a_ref[0, 1, j]], axis=0)
        y = _dot(fb_ref[j], d)
        fr_ref[0, :, j * c:(j + 1) * c] = y[:FFT_N2].astype(BF16)
        fi_ref[0, :, j * c:(j + 1) * c] = y[FFT_N2:].astype(BF16)


def _fft_stage_b(a5, fb):
    b, _, n1, n2, c = a5.shape
    kg = FFT_KG
    return pl.pallas_call(
        _fft_b_kernel,
        grid=(n1 // kg, b),
        in_specs=[pl.BlockSpec((kg, 2 * n2, 2 * n2), lambda g, i: (g, 0, 0)),
                  pl.BlockSpec((1, 2, kg, n2, c), lambda g, i: (i, 0, g, 0, 0))],
        out_specs=[pl.BlockSpec((1, n2, kg * c), lambda g, i: (i, 0, g))] * 2,
        out_shape=[jax.ShapeDtypeStruct((b, n2, n1 * c), BF16)] * 2,
        compiler_params=_params(2),
        name="fft_stage_b",
    )(fb, a5)


def _fold_kernel(dftc_ref, fw_ref, wo_ref, o_ref):
    kd = POOL_W + SGU_W
    o_ref[0:kd, :] = wo_ref[0:kd, :]
    t = _dot(dftc_ref[...], fw_ref[...]).astype(BF16)
    o_ref[kd:, :] = _dot(t, wo_ref[kd:, :]).astype(BF16)


def _fold(dftc, fw_bd, w_out):
    depth = fw_bd.shape[0]
    return pl.pallas_call(
        _fold_kernel,
        grid=(depth,),
        in_specs=[_const_spec((2 * FNET_W, FNET_W)),
                  pl.BlockSpec((None, FNET_W, FNET_W), lambda l: (l, 0, 0)),
                  pl.BlockSpec((None, D, D), lambda l: (l, 0, 0))],
        out_specs=pl.BlockSpec((None, MIX_K, D), lambda l: (l, 0, 0)),
        out_shape=jax.ShapeDtypeStruct((depth, MIX_K, D), BF16),
        compiler_params=_params(1),
        name="fold_fourier",
    )(dftc, fw_bd, w_out)


def _mix_out_kernel(x_ref, za_ref, zp_ref, zn_ref, zu_ref, zv_ref, fr_ref, fi_ref,
                    pw_ref, ps_ref, sw_ref, sb_ref, wc_ref, g_ref, edge_ref, o_ref, ext_ref, ym_ref, *, seq):
    t = MIX_TILE
    hl = POOL_HALO
    rows = t // MIX_SUB
    ext_rows = rows + 2 * hl
    j = pl.program_id(1)
    nj = seq // t
    ext_ref[0:hl] = jnp.where(j > 0, zp_ref[...].astype(F32), 0.0)
    ext_ref[hl:hl + t] = za_ref[...].astype(F32)
    ext_ref[hl + t:2 * hl + t] = jnp.where(j < nj - 1, zn_ref[...].astype(F32), 0.0)

    lane = lax.broadcasted_iota(jnp.int32, (1, 128), 1)
    low = lane < 64
    inv_win = jnp.concatenate([jnp.where(low, 1.0 / 2, 1.0 / 4), jnp.where(low, 1.0 / 8, 1.0 / 16)], axis=1)
    first_scale = jnp.where(j == 0, edge_ref[0], inv_win)
    last_scale = jnp.where(j == nj - 1, edge_ref[1], inv_win)
    nch = rows // SGU_CHUNK
    hdim = SGU_W // SGU_HEADS

    def down(a, k):
        return pltpu.roll(a, k, 0)

    def up(a, k):
        return pltpu.roll(a, ext_rows - k, 0)

    for sub in range(MIX_SUB):
        r0 = sub * rows
        r = slice(r0, r0 + rows)
        e = ext_ref[r0:r0 + ext_rows, :]
        c2 = e + down(e, 1)
        c4 = c2 + down(c2, 2)
        c4b = c4[:, 128:]
        c8 = c4b + down(c4b, 4)
        c16 = c8 + down(c8, 8)
        s0 = jnp.where(low, c2[:, :128], up(c4[:, :128], 1))
        s1 = jnp.where(low, up(c8, 3), up(c16, 7))
        ws = jnp.concatenate([s0, s1], axis=1)[hl:hl + rows]
        scale_head = first_scale if sub == 0 else inv_win
        scale_tail = last_scale if sub == MIX_SUB - 1 else inv_win
        pooled = jnp.concatenate([ws[:hl] * scale_head, ws[hl:rows - hl] * inv_win,
                                  ws[rows - hl:] * scale_tail], axis=0)
        diff = (pooled - e[hl:hl + rows]).astype(BF16)
        ya = _dot(diff, pw_ref[...]) * ps_ref[...]
        ym_ref[r, 0:POOL_W] = ya.astype(BF16)

        for h in range(SGU_HEADS):
            cols = slice(h * hdim, (h + 1) * hdim)
            vh = jnp.concatenate(
                [zv_ref[r0 + c * SGU_CHUNK:r0 + (c + 1) * SGU_CHUNK, cols] for c in range(nch)], axis=1)
            mh = _dot(sw_ref[h], vh)
            for c in range(nch):
                rc = slice(r0 + c * SGU_CHUNK, r0 + (c + 1) * SGU_CHUNK)
                mixed = mh[:, c * hdim:(c + 1) * hdim] + sb_ref[h]
                ym_ref[rc, POOL_W + h * hdim:POOL_W + (h + 1) * hdim] = (
                    zu_ref[rc, cols].astype(F32) * mixed).astype(BF16)

        ym_ref[r, POOL_W + SGU_W:POOL_W + SGU_W + FNET_W] = fr_ref[r, :]
        ym_ref[r, POOL_W + SGU_W + FNET_W:] = fi_ref[r, :]
        y = _dot(ym_ref[r, :], wc_ref[...])
        o_ref[r, :] = x_ref[r, :] + _rms(y, g_ref[...])


def _mix_out(x2, za, zu, zv, fr, fi, pw_bd, ps, sw, sb_full, wcat, g, edge, layer, *, batch, seq):
    n = x2.shape[0]
    t = MIX_TILE
    hl = POOL_HALO
    nj = seq // t
    row = lambda w_: pl.BlockSpec((t, w_), lambda b, j: (b * nj + j, 0))
    prev = pl.BlockSpec((hl, POOL_W), lambda b, j: (jnp.maximum((b * nj + j) * (t // hl) - 1, 0), 0))
    nxt = pl.BlockSpec((hl, POOL_W), lambda b, j: (jnp.minimum((b * nj + j + 1) * (t // hl), n // hl - 1), 0))
    return pl.pallas_call(
        functools.partial(_mix_out_kernel, seq=seq),
        grid=(batch, nj),
        in_specs=[row(D), row(POOL_W), prev, nxt, row(SGU_W), row(SGU_W), row(FNET_W), row(FNET_W),
                  _layer_spec((POOL_W, POOL_W), layer), _layer_spec((1, POOL_W), layer),
                  _layer_spec((SGU_HEADS, SGU_CHUNK, SGU_CHUNK), layer),
                  _layer_spec((SGU_HEADS, SGU_CHUNK, SGU_CHUNK), layer),
                  _layer_spec((MIX_K, D), layer), _layer_spec((1, D), layer),
                  _const_spec((2, hl, POOL_W))],
        out_specs=row(D),
        out_shape=jax.ShapeDtypeStruct((n, D), F32),
        scratch_shapes=[pltpu.VMEM((t + 2 * hl, POOL_W), F32), pltpu.VMEM((t, MIX_K), BF16)],
        compiler_params=_params(2),
        name="mix_out",
    )(x2, za, za, za, zu, zv, fr, fi, pw_bd, ps, sw, sb_full, wcat, g, edge)


def _kv_kernel(m_ref, g_ref, wk_ref, wv_ref, k_ref, v_ref):
    m = _rms(m_ref[0], g_ref[...]).astype(BF16)
    k_ref[...] = _dot(m, wk_ref[...]).astype(BF16)
    v_ref[...] = _dot(m, wv_ref[...]).astype(BF16)


def _kv_proj(mem, g, wk, wv):
    b = mem.shape[0]
    depth = wk.shape[0]
    per_layer = lambda shape: pl.BlockSpec((None,) + shape, lambda l, i: (l,) + (0,) * len(shape))
    out = pl.BlockSpec((None, None, MEM_LEN, D), lambda l, i: (l, i, 0, 0))
    return pl.pallas_call(
        _kv_kernel,
        grid=(depth, b),
        in_specs=[pl.BlockSpec((1, MEM_LEN, D), lambda l, i: (i, 0, 0)),
                  per_layer((1, D)), per_layer((D, D)), per_layer((D, D))],
        out_specs=[out, out],
        out_shape=[jax.ShapeDtypeStruct((depth, b, MEM_LEN, D), BF16)] * 2,
        compiler_params=_params(2),
        name="kv_proj",
    )(mem, g, wk, wv)


def _attn_kernel(x_ref, k_ref, v_ref, gpre_ref, wq_ref, wo_ref, gpost_ref, o_ref, oh_ref):
    rows = x_ref.shape[0] // ATT_SUB
    for sub in range(ATT_SUB):
        r = slice(sub * rows, (sub + 1) * rows)
        x = x_ref[r, :]
        h = _rms(x, gpre_ref[...]).astype(BF16)
        q = (_dot(h, wq_ref[...]) * (XA_HD ** -0.5)).astype(BF16)
        for hd in range(XA_HEADS):
            sl = slice(hd * XA_HD, (hd + 1) * XA_HD)
            s = lax.dot_general(q[:, sl], k_ref[:, sl], (((1,), (1,)), ((), ())),
                                preferred_element_type=F32)
            e = jnp.exp(s - jnp.max(s, axis=-1, keepdims=True))
            p = (e / jnp.sum(e, axis=-1, keepdims=True)).astype(BF16)
            oh_ref[r, sl] = _dot(p, v_ref[:, sl]).astype(BF16)
        y = _dot(oh_ref[r, :], wo_ref[...])
        o_ref[r, :] = x + _rms(y, gpost_ref[...])


def _attention(x2, k, v, gpre, wq, wo, gpost, layer, *, batch, seq):
    n = x2.shape[0]
    t = ATT_TILE
    nj = seq // t
    row = pl.BlockSpec((t, D), lambda b, j: (b * nj + j, 0))
    kv = pl.BlockSpec((None, None, MEM_LEN, D), lambda b, j: (layer, b, 0, 0))
    return pl.pallas_call(
        _attn_kernel,
        grid=(batch, nj),
        in_specs=[row, kv, kv, _layer_spec((1, D), layer), _layer_spec((D, D), layer),
                  _layer_spec((D, D), layer), _layer_spec((1, D), layer)],
        out_specs=row,
        out_shape=jax.ShapeDtypeStruct((n, D), F32),
        scratch_shapes=[pltpu.VMEM((t, D), BF16)],
        compiler_params=_params(2),
        name="mem_attention",
    )(x2, k, v, gpre, wq, wo, gpost)


def _ffn_kernel(x_ref, gpre_ref, wg_ref, wu_ref, wd_ref, gpost_ref, o_ref):
    rows = x_ref.shape[0] // FFN_SUB
    for sub in range(FFN_SUB):
        r = slice(sub * rows, (sub + 1) * rows)
        x = x_ref[r, :]
        h = _rms(x, gpre_ref[...]).astype(BF16)
        y = None
        off = 0
        for ch in FFN_CHUNKS:
            g = _dot(h, wg_ref[:, off:off + ch])
            u = _dot(h, wu_ref[:, off:off + ch])
            a = (g * jax.nn.sigmoid(g) * u).astype(BF16)
            part = _dot(a, wd_ref[off:off + ch, :])
            y = part if y is None else y + part
            off += ch
        o_ref[r, :] = x + _rms(y, gpost_ref[...])


def _ffn(x2, gpre, wg, wu, wd, gpost, layer):
    n = x2.shape[0]
    t = FFN_TILE
    row = pl.BlockSpec((t, D), lambda i: (i, 0))
    return pl.pallas_call(
        _ffn_kernel,
        grid=(n // t,),
        in_specs=[row, _layer_spec((1, D), layer), _layer_spec((D, FFN_H), layer),
                  _layer_spec((D, FFN_H), layer), _layer_spec((FFN_H, D), layer),
                  _layer_spec((1, D), layer)],
        out_specs=row,
        out_shape=jax.ShapeDtypeStruct((n, D), F32),
        compiler_params=_params(1),
        name="swiglu",
    )(x2, gpre, wg, wu, wd, gpost)


def kernel(x, mem, ln_mix_pre, w_in, pool_w, pool_scale, sgu_norm, sgu_w, sgu_b, fnet_w, w_out,
           ln_mix_post, ln_xa_pre, ln_mem, xa_wq, xa_wk, xa_wv, xa_wo, ln_xa_post, ln_ffn_pre,
           ffn_wg, ffn_wu, ffn_wd, ln_ffn_post):
    batch, seq, d = x.shape
    depth = w_in.shape[0]
    assert d == D and sum(FFN_CHUNKS) == FFN_H
    assert all(seq % t == 0 for t in (IN_TILE, MIX_TILE, ATT_TILE, FFN_TILE))
    assert (MIX_TILE // MIX_SUB) % SGU_CHUNK == 0
    fa, fb, dftc = (jnp.asarray(t).astype(BF16) for t in _fft_tables(seq))
    edge = jnp.asarray(_pool_edge_scales(seq))
    n = batch * seq
    x2 = x.reshape(n, d)
    vec = lambda a: a.reshape(depth, 1, -1).astype(F32)
    bf = lambda a: a.astype(BF16)

    g_mix_pre, g_mix_post, g_xa_pre, g_xa_post = vec(ln_mix_pre), vec(ln_mix_post), vec(ln_xa_pre), vec(ln_xa_post)
    g_ffn_pre, g_ffn_post, g_mem = vec(ln_ffn_pre), vec(ln_ffn_post), vec(ln_mem)
    sg, ps = vec(sgu_norm), vec(pool_scale)
    w_in_b = bf(w_in)
    wq_b, wk_b, wv_b, wo_b = bf(xa_wq), bf(xa_wk), bf(xa_wv), bf(xa_wo)
    wg_b, wu_b, wd_b = bf(ffn_wg), bf(ffn_wu), bf(ffn_wd)
    pw_bd = bf(_block_diag(pool_w))
    sw_b = bf(sgu_w)
    sb_full = jnp.broadcast_to(sgu_b[:, :, :, None], sgu_b.shape + (SGU_CHUNK,)).astype(F32)
    wcat = _fold(dftc, bf(_block_diag(fnet_w)), bf(w_out))
    k_all, v_all = _kv_proj(mem, g_mem, wk_b, wv_b)

    for l in range(depth):
        za, zu, zv, zc = _in_proj(x2, g_mix_pre, w_in_b, sg, l)
        a = _fft_stage_a(zc.reshape(batch, FFT_N1, FFT_N2 * FNET_W), fa)
        fr, fi = _fft_stage_b(a.reshape(batch, 2, FFT_N1, FFT_N2, FNET_W), fb)
        x2 = _mix_out(x2, za, zu, zv, fr.reshape(n, FNET_W), fi.reshape(n, FNET_W), pw_bd, ps, sw_b, sb_full,
                      wcat, g_mix_post, edge, l, batch=batch, seq=seq)
        x2 = _attention(x2, k_all, v_all, g_xa_pre, wq_b, wo_b, g_xa_post, l, batch=batch, seq=seq)
        x2 = _ffn(x2, g_ffn_pre, wg_b, wu_b, wd_b, g_ffn_post, l)
    return x2.reshape(batch, seq, d)
```

```python
import functools

import numpy as np
import jax
import jax.numpy as jnp
from jax import lax
from jax.experimental import pallas as pl
from jax.experimental.pallas import tpu as pltpu

F32 = jnp.float32
BF16 = jnp.bfloat16

D = 1024
POOL_W = 256
POOL_WINDOWS = (2, 4, 8, 16)
SGU_W = 512
FNET_W = 256
IN_W = POOL_W + 2 * SGU_W + FNET_W
MIX_K = POOL_W + SGU_W + 2 * FNET_W
SGU_CHUNK = 128
SGU_HEADS = 4
MEM_LEN = 256
XA_HEADS = 4
XA_HD = D // XA_HEADS
FFN_H = 2816
RMS_EPS = 1e-6
LN_EPS = 1e-5

FFT_N1 = 64
FFT_N2 = 128
ROW_TILE = 16
POOL_HALO = 16
IN_TILE, IN_SUB = 2048, 4
MIX_TILE, MIX_SUB = 1024, 2
ATT_TILE, ATT_SUB = 1024, 2
FFN_TILE, FFN_SUB = 1024, 2
FFN_CHUNKS = (768, 768, 768, 512)
VMEM_LIMIT = 56 * 1024 * 1024


def _rms(xf, g):
    ms = jnp.mean(xf * xf, axis=-1, keepdims=True)
    return xf * lax.rsqrt(ms + RMS_EPS) * g


def _dot(a, b):
    return jnp.dot(a, b, preferred_element_type=F32)


def _params(n_axes):
    return pltpu.CompilerParams(
        dimension_semantics=("arbitrary",) * n_axes, vmem_limit_bytes=VMEM_LIMIT)


def _const_spec(shape):
    nd = len(shape)
    return pl.BlockSpec(shape, lambda *_: (0,) * nd, pipeline_mode=pl.Buffered(1))


def _layer_spec(shape, layer):
    nd = len(shape)
    return pl.BlockSpec((None,) + tuple(shape), lambda *_: (layer,) + (0,) * nd,
                        pipeline_mode=pl.Buffered(1))


@functools.lru_cache(maxsize=None)
def _fft_tables(seq):
    assert seq == FFT_N1 * FFT_N2
    two_pi = 2.0 * np.pi
    k1 = np.arange(FFT_N1)
    ang = two_pi * ((k1[:, None] * k1[None, :]) % FFT_N1) / FFT_N1
    fa = np.concatenate([np.cos(ang), -np.sin(ang)], axis=0) / np.sqrt(FFT_N1)
    k2 = np.arange(FFT_N2)
    k = k1[:, None, None] + FFT_N1 * k2[None, :, None]
    ph = two_pi * ((k * k2[None, None, :]) % seq) / seq
    c, s = np.cos(ph), np.sin(ph)
    fb = np.concatenate([np.concatenate([c, s], axis=2),
                         np.concatenate([-s, c], axis=2)], axis=1) / np.sqrt(FFT_N2)
    hd = FNET_W // 4
    cc = np.arange(hd)
    ang = two_pi * ((cc[:, None] * cc[None, :]) % hd) / hd
    eye = np.eye(4)
    dftc = np.concatenate([np.kron(eye, np.cos(ang)), np.kron(eye, np.sin(ang))], axis=0) / np.sqrt(hd)
    ka = np.kron(fa, np.eye(ROW_TILE))
    o = np.arange(ROW_TILE * ROW_TILE)
    swap = np.eye(ROW_TILE * ROW_TILE)[(o % ROW_TILE) * ROW_TILE + o // ROW_TILE]
    return tuple(np.asarray(t, dtype=np.float32) for t in (ka, fb, dftc, swap))


@functools.lru_cache(maxsize=None)
def _pool_edge_scales(seq):
    hl = POOL_HALO
    pos = np.concatenate([np.arange(hl), np.arange(seq - hl, seq)])[:, None]
    win = np.repeat(np.asarray(POOL_WINDOWS), POOL_W // len(POOL_WINDOWS))[None, :]
    left = win // 2
    right = win - 1 - left
    cnt = np.minimum(pos + right, seq - 1) - np.maximum(pos - left, 0) + 1
    return (1.0 / cnt).astype(np.float32).reshape(2, hl, POOL_W)


def _block_diag(w):
    l, g, c, _ = w.shape
    eye = jnp.eye(g, dtype=w.dtype)
    return (eye[None, :, None, :, None] * w[:, :, :, None, :]).reshape(l, g * c, g * c)


def _in_proj_kernel(x_ref, g_ref, w_ref, sg_ref, za_ref, zu_ref, zv_ref, zc_ref):
    rows = x_ref.shape[0] // IN_SUB
    for sub in range(IN_SUB):
        r = slice(sub * rows, (sub + 1) * rows)
        h = _rms(x_ref[r, :], g_ref[...]).astype(BF16)
        z = _dot(h, w_ref[...])
        za_ref[r, :] = z[:, :POOL_W].astype(BF16)
        zu_ref[r, :] = z[:, POOL_W:POOL_W + SGU_W].astype(BF16)
        v = z[:, POOL_W + SGU_W:POOL_W + 2 * SGU_W]
        vc = v - jnp.mean(v, axis=-1, keepdims=True)
        vn = vc * lax.rsqrt(jnp.mean(vc * vc, axis=-1, keepdims=True) + LN_EPS) * sg_ref[...]
        zv_ref[r, :] = vn.astype(BF16)
        zc_ref[r, :] = z[:, POOL_W + 2 * SGU_W:].astype(BF16)


def _in_proj(x2, g, w, sg, layer):
    n = x2.shape[0]
    t = IN_TILE
    row = lambda w_: pl.BlockSpec((t, w_), lambda i: (i, 0))
    return pl.pallas_call(
        _in_proj_kernel,
        grid=(n // t,),
        in_specs=[row(D), _layer_spec((1, D), layer), _layer_spec((D, IN_W), layer),
                  _layer_spec((1, SGU_W), layer)],
        out_specs=[row(POOL_W), row(SGU_W), row(SGU_W), row(FNET_W)],
        out_shape=[jax.ShapeDtypeStruct((n, POOL_W), BF16), jax.ShapeDtypeStruct((n, SGU_W), BF16),
                   jax.ShapeDtypeStruct((n, SGU_W), BF16), jax.ShapeDtypeStruct((n, FNET_W), BF16)],
        compiler_params=_params(1),
        name="in_proj",
    )(x2, g, w, sg)


def _fft_a_kernel(ka_ref, x_ref, o_ref):
    rt = ROW_TILE
    c = x_ref.shape[-1]
    x = x_ref[0].reshape(FFT_N1 * rt, c)
    a = _dot(ka_ref[...], x).astype(BF16)
    o_ref[0] = a.reshape(2, FFT_N1, rt, c)


def _fft_stage_a(x4, ka):
    b, n1, n2, c = x4.shape
    rt = ROW_TILE
    return pl.pallas_call(
        _fft_a_kernel,
        grid=(b, n2 // rt),
        in_specs=[_const_spec((2 * n1 * rt, n1 * rt)),
                  pl.BlockSpec((1, n1, rt, c), lambda i, g: (i, 0, g, 0))],
        out_specs=pl.BlockSpec((1, 2, n1, rt, c), lambda i, g: (i, 0, 0, g, 0)),
        out_shape=jax.ShapeDtypeStruct((b, 2, n1, n2, c), BF16),
        compiler_params=_params(2),
        name="fft_stage_a",
    )(ka, x4)


def _fft_b_kernel(fb_ref, swap_ref, a_ref, fr_ref, fi_ref, y_ref):
    rt = ROW_TILE
    c = a_ref.shape[-1]
    for j in range(rt):
        d = jnp.concatenate([a_ref[0, 0, j], a_ref[0, 1, j]], axis=0)
        y_ref[j] = _dot(fb_ref[j], d).astype(BF16)
    for t in range(FFT_N2 // rt):
        re = y_ref[:, t * rt:(t + 1) * rt, :].reshape(rt * rt, c)
        im = y_ref[:, FFT_N2 + t * rt:FFT_N2 + (t + 1) * rt, :].reshape(rt * rt, c)
        q = _dot(swap_ref[...], jnp.concatenate([re, im], axis=1)).astype(BF16)
        fr_ref[0, t * rt:(t + 1) * rt] = q[:, :c].reshape(rt, rt, c)
        fi_ref[0, t * rt:(t + 1) * rt] = q[:, c:].reshape(rt, rt, c)


def _fft_stage_b(a5, fb, swap):
    b, _, n1, n2, c = a5.shape
    rt = ROW_TILE
    out = pl.BlockSpec((1, n2, rt, c), lambda g, i: (i, 0, g, 0))
    return pl.pallas_call(
        _fft_b_kernel,
        grid=(n1 // rt, b),
        in_specs=[pl.BlockSpec((rt, 2 * n2, 2 * n2), lambda g, i: (g, 0, 0)),
                  _const_spec((rt * rt, rt * rt)),
                  pl.BlockSpec((1, 2, rt, n2, c), lambda g, i: (i, 0, g, 0, 0))],
        out_specs=[out, out],
        out_shape=[jax.ShapeDtypeStruct((b, n2, n1, c), BF16)] * 2,
        scratch_shapes=[pltpu.VMEM((rt, 2 * n2, c), BF16)],
        compiler_params=_params(2),
        name="fft_stage_b",
    )(fb, swap, a5)


def _fold_kernel(dftc_ref, fw_ref, wo_ref, o_ref):
    kd = POOL_W + SGU_W
    o_ref[0:kd, :] = wo_ref[0:kd, :]
    t = _dot(dftc_ref[...], fw_ref[...]).astype(BF16)
    o_ref[kd:, :] = _dot(t, wo_ref[kd:, :]).astype(BF16)


def _fold(dftc, fw_bd, w_out):
    depth = fw_bd.shape[0]
    return pl.pallas_call(
        _fold_kernel,
        grid=(depth,),
        in_specs=[_const_spec((2 * FNET_W, FNET_W)),
                  pl.BlockSpec((None, FNET_W, FNET_W), lambda l: (l, 0, 0)),
                  pl.BlockSpec((None, D, D), lambda l: (l, 0, 0))],
        out_specs=pl.BlockSpec((None, MIX_K, D), lambda l: (l, 0, 0)),
        out_shape=jax.ShapeDtypeStruct((depth, MIX_K, D), BF16),
        compiler_params=_params(1),
        name="fold_fourier",
    )(dftc, fw_bd, w_out)


def _mix_out_kernel(x_ref, za_ref, zp_ref, zn_ref, zu_ref, zv_ref, fr_ref, fi_ref,
                    pw_ref, ps_ref, sw_ref, sb_ref, wc_ref, g_ref, edge_ref, o_ref, ext_ref, ym_ref, *, seq):
    t = MIX_TILE
    hl = POOL_HALO
    rows = t // MIX_SUB
    ext_rows = rows + 2 * hl
    j = pl.program_id(1)
    nj = seq // t
    ext_ref[0:hl] = jnp.where(j > 0, zp_ref[...].astype(F32), 0.0)
    ext_ref[hl:hl + t] = za_ref[...].astype(F32)
    ext_ref[hl + t:2 * hl + t] = jnp.where(j < nj - 1, zn_ref[...].astype(F32), 0.0)

    lane = lax.broadcasted_iota(jnp.int32, (1, 128), 1)
    low = lane < 64
    inv_win = jnp.concatenate([jnp.where(low, 1.0 / 2, 1.0 / 4), jnp.where(low, 1.0 / 8, 1.0 / 16)], axis=1)
    first_scale = jnp.where(j == 0, edge_ref[0], inv_win)
    last_scale = jnp.where(j == nj - 1, edge_ref[1], inv_win)
    nch = rows // SGU_CHUNK
    hdim = SGU_W // SGU_HEADS

    def down(a, k):
        return pltpu.roll(a, k, 0)

    def up(a, k):
        return pltpu.roll(a, ext_rows - k, 0)

    for sub in range(MIX_SUB):
        r0 = sub * rows
        r = slice(r0, r0 + rows)
        e = ext_ref[r0:r0 + ext_rows, :]
        c2 = e + down(e, 1)
        c4 = c2 + down(c2, 2)
        c4b = c4[:, 128:]
        c8 = c4b + down(c4b, 4)
        c16 = c8 + down(c8, 8)
        s0 = jnp.where(low, c2[:, :128], up(c4[:, :128], 1))
        s1 = jnp.where(low, up(c8, 3), up(c16, 7))
        ws = jnp.concatenate([s0, s1], axis=1)[hl:hl + rows]
        scale_head = first_scale if sub == 0 else inv_win
        scale_tail = last_scale if sub == MIX_SUB - 1 else inv_win
        pooled = jnp.concatenate([ws[:hl] * scale_head, ws[hl:rows - hl] * inv_win,
                                  ws[rows - hl:] * scale_tail], axis=0)
        diff = (pooled - e[hl:hl + rows]).astype(BF16)
        ya = _dot(diff, pw_ref[...]) * ps_ref[...]
        ym_ref[r, 0:POOL_W] = ya.astype(BF16)

        for h in range(SGU_HEADS):
            cols = slice(h * hdim, (h + 1) * hdim)
            vh = jnp.concatenate(
                [zv_ref[r0 + c * SGU_CHUNK:r0 + (c + 1) * SGU_CHUNK, cols] for c in range(nch)], axis=1)
            mh = _dot(sw_ref[h], vh)
            for c in range(nch):
                rc = slice(r0 + c * SGU_CHUNK, r0 + (c + 1) * SGU_CHUNK)
                mixed = mh[:, c * hdim:(c + 1) * hdim] + sb_ref[h]
                ym_ref[rc, POOL_W + h * hdim:POOL_W + (h + 1) * hdim] = (
                    zu_ref[rc, cols].astype(F32) * mixed).astype(BF16)

        ym_ref[r, POOL_W + SGU_W:POOL_W + SGU_W + FNET_W] = fr_ref[r, :]
        ym_ref[r, POOL_W + SGU_W + FNET_W:] = fi_ref[r, :]
        y = _dot(ym_ref[r, :], wc_ref[...])
        o_ref[r, :] = x_ref[r, :] + _rms(y, g_ref[...])


def _mix_out(x2, za, zu, zv, fr, fi, pw_bd, ps, sw, sb_full, wcat, g, edge, layer, *, batch, seq):
    n = x2.shape[0]
    t = MIX_TILE
    hl = POOL_HALO
    nj = seq // t
    row = lambda w_: pl.BlockSpec((t, w_), lambda b, j: (b * nj + j, 0))
    prev = pl.BlockSpec((hl, POOL_W), lambda b, j: (jnp.maximum((b * nj + j) * (t // hl) - 1, 0), 0))
    nxt = pl.BlockSpec((hl, POOL_W), lambda b, j: (jnp.minimum((b * nj + j + 1) * (t // hl), n // hl - 1), 0))
    return pl.pallas_call(
        functools.partial(_mix_out_kernel, seq=seq),
        grid=(batch, nj),
        in_specs=[row(D), row(POOL_W), prev, nxt, row(SGU_W), row(SGU_W), row(FNET_W), row(FNET_W),
                  _layer_spec((POOL_W, POOL_W), layer), _layer_spec((1, POOL_W), layer),
                  _layer_spec((SGU_HEADS, SGU_CHUNK, SGU_CHUNK), layer),
                  _layer_spec((SGU_HEADS, SGU_CHUNK, SGU_CHUNK), layer),
                  _layer_spec((MIX_K, D), layer), _layer_spec((1, D), layer),
                  _const_spec((2, hl, POOL_W))],
        out_specs=row(D),
        out_shape=jax.ShapeDtypeStruct((n, D), F32),
        scratch_shapes=[pltpu.VMEM((t + 2 * hl, POOL_W), F32), pltpu.VMEM((t, MIX_K), BF16)],
        compiler_params=_params(2),
        name="mix_out",
    )(x2, za, za, za, zu, zv, fr, fi, pw_bd, ps, sw, sb_full, wcat, g, edge)


def _kv_kernel(m_ref, g_ref, wk_ref, wv_ref, k_ref, v_ref):
    m = _rms(m_ref[0], g_ref[...]).astype(BF16)
    k_ref[...] = _dot(m, wk_ref[...]).astype(BF16)
    v_ref[...] = _dot(m, wv_ref[...]).astype(BF16)


def _kv_proj(mem, g, wk, wv):
    b = mem.shape[0]
    depth = wk.shape[0]
    per_layer = lambda shape: pl.BlockSpec((None,) + shape, lambda l, i: (l,) + (0,) * len(shape))
    out = pl.BlockSpec((None, None, MEM_LEN, D), lambda l, i: (l, i, 0, 0))
    return pl.pallas_call(
        _kv_kernel,
        grid=(depth, b),
        in_specs=[pl.BlockSpec((1, MEM_LEN, D), lambda l, i: (i, 0, 0)),
                  per_layer((1, D)), per_layer((D, D)), per_layer((D, D))],
        out_specs=[out, out],
        out_shape=[jax.ShapeDtypeStruct((depth, b, MEM_LEN, D), BF16)] * 2,
        compiler_params=_params(2),
        name="kv_proj",
    )(mem, g, wk, wv)


def _attn_kernel(x_ref, k_ref, v_ref, gpre_ref, wq_ref, wo_ref, gpost_ref, o_ref, oh_ref):
    rows = x_ref.shape[0] // ATT_SUB
    for sub in range(ATT_SUB):
        r = slice(sub * rows, (sub + 1) * rows)
        x = x_ref[r, :]
        h = _rms(x, gpre_ref[...]).astype(BF16)
        q = (_dot(h, wq_ref[...]) * (XA_HD ** -0.5)).astype(BF16)
        for hd in range(XA_HEADS):
            sl = slice(hd * XA_HD, (hd + 1) * XA_HD)
            s = lax.dot_general(q[:, sl], k_ref[:, sl], (((1,), (1,)), ((), ())),
                                preferred_element_type=F32)
            e = jnp.exp(s - jnp.max(s, axis=-1, keepdims=True))
            p = (e / jnp.sum(e, axis=-1, keepdims=True)).astype(BF16)
            oh_ref[r, sl] = _dot(p, v_ref[:, sl]).astype(BF16)
        y = _dot(oh_ref[r, :], wo_ref[...])
        o_ref[r, :] = x + _rms(y, gpost_ref[...])


def _attention(x2, k, v, gpre, wq, wo, gpost, layer, *, batch, seq):
    n = x2.shape[0]
    t = ATT_TILE
    nj = seq // t
    row = pl.BlockSpec((t, D), lambda b, j: (b * nj + j, 0))
    kv = pl.BlockSpec((None, None, MEM_LEN, D), lambda b, j: (layer, b, 0, 0))
    return pl.pallas_call(
        _attn_kernel,
        grid=(batch, nj),
        in_specs=[row, kv, kv, _layer_spec((1, D), layer), _layer_spec((D, D), layer),
                  _layer_spec((D, D), layer), _layer_spec((1, D), layer)],
        out_specs=row,
        out_shape=jax.ShapeDtypeStruct((n, D), F32),
        scratch_shapes=[pltpu.VMEM((t, D), BF16)],
        compiler_params=_params(2),
        name="mem_attention",
    )(x2, k, v, gpre, wq, wo, gpost)


def _ffn_kernel(x_ref, gpre_ref, wg_ref, wu_ref, wd_ref, gpost_ref, o_ref):
    rows = x_ref.shape[0] // FFN_SUB
    for sub in range(FFN_SUB):
        r = slice(sub * rows, (sub + 1) * rows)
        x = x_ref[r, :]
        h = _rms(x, gpre_ref[...]).astype(BF16)
        y = None
        off = 0
        for ch in FFN_CHUNKS:
            g = _dot(h, wg_ref[:, off:off + ch])
            u = _dot(h, wu_ref[:, off:off + ch])
            a = (g * jax.nn.sigmoid(g) * u).astype(BF16)
            part = _dot(a, wd_ref[off:off + ch, :])
            y = part if y is None else y + part
            off += ch
        o_ref[r, :] = x + _rms(y, gpost_ref[...])


def _ffn(x2, gpre, wg, wu, wd, gpost, layer):
    n = x2.shape[0]
    t = FFN_TILE
    row = pl.BlockSpec((t, D), lambda i: (i, 0))
    return pl.pallas_call(
        _ffn_kernel,
        grid=(n // t,),
        in_specs=[row, _layer_spec((1, D), layer), _layer_spec((D, FFN_H), layer),
                  _layer_spec((D, FFN_H), layer), _layer_spec((FFN_H, D), layer),
                  _layer_spec((1, D), layer)],
        out_specs=row,
        out_shape=jax.ShapeDtypeStruct((n, D), F32),
        compiler_params=_params(1),
        name="swiglu",
    )(x2, gpre, wg, wu, wd, gpost)


def kernel(x, mem, ln_mix_pre, w_in, pool_w, pool_scale, sgu_norm, sgu_w, sgu_b, fnet_w, w_out,
           ln_mix_post, ln_xa_pre, ln_mem, xa_wq, xa_wk, xa_wv, xa_wo, ln_xa_post, ln_ffn_pre,
           ffn_wg, ffn_wu, ffn_wd, ln_ffn_post):
    batch, seq, d = x.shape
    depth = w_in.shape[0]
    assert d == D and sum(FFN_CHUNKS) == FFN_H
    assert all(seq % t == 0 for t in (IN_TILE, MIX_TILE, ATT_TILE, FFN_TILE))
    assert (MIX_TILE // MIX_SUB) % SGU_CHUNK == 0
    ka, fb, dftc, swap = (jnp.asarray(t).astype(BF16) for t in _fft_tables(seq))
    edge = jnp.asarray(_pool_edge_scales(seq))
    n = batch * seq
    x2 = x.reshape(n, d)
    vec = lambda a: a.reshape(depth, 1, -1).astype(F32)
    bf = lambda a: a.astype(BF16)

    g_mix_pre, g_mix_post, g_xa_pre, g_xa_post = vec(ln_mix_pre), vec(ln_mix_post), vec(ln_xa_pre), vec(ln_xa_post)
    g_ffn_pre, g_ffn_post, g_mem = vec(ln_ffn_pre), vec(ln_ffn_post), vec(ln_mem)
    sg, ps = vec(sgu_norm), vec(pool_scale)
    w_in_b = bf(w_in)
    wq_b, wk_b, wv_b, wo_b = bf(xa_wq), bf(xa_wk), bf(xa_wv), bf(xa_wo)
    wg_b, wu_b, wd_b = bf(ffn_wg), bf(ffn_wu), bf(ffn_wd)
    pw_bd = bf(_block_diag(pool_w))
    sw_b = bf(sgu_w)
    sb_full = jnp.broadcast_to(sgu_b[:, :, :, None], sgu_b.shape + (SGU_CHUNK,)).astype(F32)
    wcat = _fold(dftc, bf(_block_diag(fnet_w)), bf(w_out))
    k_all, v_all = _kv_proj(mem, g_mem, wk_b, wv_b)

    for l in range(depth):
        za, zu, zv, zc = _in_proj(x2, g_mix_pre, w_in_b, sg, l)
        a5 = _fft_stage_a(zc.reshape(batch, FFT_N1, FFT_N2, FNET_W), ka)
        fr, fi = _fft_stage_b(a5, fb, swap)
        x2 = _mix_out(x2, za, zu, zv, fr.reshape(n, FNET_W), fi.reshape(n, FNET_W), pw_bd, ps, sw_b, sb_full,
                      wcat, g_mix_post, edge, l, batch=batch, seq=seq)
        x2 = _attention(x2, k_all, v_all, g_xa_pre, wq_b, wo_b, g_xa_post, l, batch=batch, seq=seq)
        x2 = _ffn(x2, g_ffn_pre, wg_b, wu_b, wd_b, g_ffn_post, l)
    return x2.reshape(batch, seq, d)
```

```python
import functools

import numpy as np
import jax
import jax.numpy as jnp
from jax import lax
from jax.experimental import pallas as pl
from jax.experimental.pallas import tpu as pltpu

F32 = jnp.float32
BF16 = jnp.bfloat16

D = 1024
POOL_W = 256
POOL_WINDOWS = (2, 4, 8, 16)
SGU_W = 512
FNET_W = 256
IN_W = POOL_W + 2 * SGU_W + FNET_W
MIX_K = POOL_W + SGU_W + 2 * FNET_W
SGU_CHUNK = 128
SGU_HEADS = 4
MEM_LEN = 256
XA_HEADS = 4
XA_HD = D // XA_HEADS
FFN_H = 2816
RMS_EPS = 1e-6
LN_EPS = 1e-5

FFT_N1 = 64
FFT_N2 = 128
ROW_TILE = 16
POOL_HALO = 16
IN_TILE, IN_SUB = 2048, 4
MIX_TILE, MIX_SUB = 1024, 2
ATT_TILE, ATT_SUB = 2048, 4
FFN_TILE, FFN_SUB = 1024, 2
FFN_CHUNKS = (768, 768, 768, 512)
VMEM_LIMIT = 56 * 1024 * 1024


def _rms(xf, g):
    ms = jnp.mean(xf * xf, axis=-1, keepdims=True)
    return xf * lax.rsqrt(ms + RMS_EPS) * g


def _dot(a, b):
    return jnp.dot(a, b, preferred_element_type=F32)


def _params(n_axes):
    return pltpu.CompilerParams(
        dimension_semantics=("arbitrary",) * n_axes, vmem_limit_bytes=VMEM_LIMIT)


def _const_spec(shape):
    nd = len(shape)
    return pl.BlockSpec(shape, lambda *_: (0,) * nd, pipeline_mode=pl.Buffered(1))


def _layer_spec(shape, layer):
    nd = len(shape)
    return pl.BlockSpec((None,) + tuple(shape), lambda *_: (layer,) + (0,) * nd,
                        pipeline_mode=pl.Buffered(1))


def _row(ref, layer):
    return ref[layer:layer + 1, :]


@functools.lru_cache(maxsize=None)
def _fft_tables(seq):
    assert seq == FFT_N1 * FFT_N2
    two_pi = 2.0 * np.pi
    k1 = np.arange(FFT_N1)
    ang = two_pi * ((k1[:, None] * k1[None, :]) % FFT_N1) / FFT_N1
    fa = np.concatenate([np.cos(ang), -np.sin(ang)], axis=0) / np.sqrt(FFT_N1)
    k2 = np.arange(FFT_N2)
    k = k1[:, None, None] + FFT_N1 * k2[None, :, None]
    ph = two_pi * ((k * k2[None, None, :]) % seq) / seq
    c, s = np.cos(ph), np.sin(ph)
    fb = np.concatenate([np.concatenate([c, s], axis=2),
                         np.concatenate([-s, c], axis=2)], axis=1) / np.sqrt(FFT_N2)
    hd = FNET_W // 4
    cc = np.arange(hd)
    ang = two_pi * ((cc[:, None] * cc[None, :]) % hd) / hd
    eye = np.eye(4)
    dftc = np.concatenate([np.kron(eye, np.cos(ang)), np.kron(eye, np.sin(ang))], axis=0) / np.sqrt(hd)
    ka = np.kron(fa, np.eye(ROW_TILE))
    o = np.arange(ROW_TILE * ROW_TILE)
    swap = np.eye(ROW_TILE * ROW_TILE)[(o % ROW_TILE) * ROW_TILE + o // ROW_TILE]
    return tuple(np.asarray(t, dtype=np.float32) for t in (ka, fb, dftc, swap))


@functools.lru_cache(maxsize=None)
def _pool_edge_scales(seq):
    hl = POOL_HALO
    pos = np.concatenate([np.arange(hl), np.arange(seq - hl, seq)])[:, None]
    win = np.repeat(np.asarray(POOL_WINDOWS), POOL_W // len(POOL_WINDOWS))[None, :]
    left = win // 2
    right = win - 1 - left
    cnt = np.minimum(pos + right, seq - 1) - np.maximum(pos - left, 0) + 1
    return (1.0 / cnt).astype(np.float32).reshape(2, hl, POOL_W)


def _block_diag(w):
    l, g, c, _ = w.shape
    eye = jnp.eye(g, dtype=w.dtype)
    return (eye[None, :, None, :, None] * w[:, :, :, None, :]).reshape(l, g * c, g * c)


def _in_proj_kernel(x_ref, g_ref, w_ref, sg_ref, za_ref, zu_ref, zv_ref, zc_ref, wb_ref, *, layer):
    @pl.when(pl.program_id(0) == 0)
    def _():
        wb_ref[...] = w_ref[...].astype(BF16)

    g, sg = _row(g_ref, layer), _row(sg_ref, layer)
    rows = x_ref.shape[0] // IN_SUB
    for sub in range(IN_SUB):
        r = slice(sub * rows, (sub + 1) * rows)
        h = _rms(x_ref[r, :], g).astype(BF16)
        z = _dot(h, wb_ref[...])
        za_ref[r, :] = z[:, :POOL_W].astype(BF16)
        zu_ref[r, :] = z[:, POOL_W:POOL_W + SGU_W].astype(BF16)
        v = z[:, POOL_W + SGU_W:POOL_W + 2 * SGU_W]
        vc = v - jnp.mean(v, axis=-1, keepdims=True)
        vn = vc * lax.rsqrt(jnp.mean(vc * vc, axis=-1, keepdims=True) + LN_EPS) * sg
        zv_ref[r, :] = vn.astype(BF16)
        zc_ref[r, :] = z[:, POOL_W + 2 * SGU_W:].astype(BF16)


def _in_proj(x2, g, w, sg, layer):
    n = x2.shape[0]
    t = IN_TILE
    row = lambda w_: pl.BlockSpec((t, w_), lambda i: (i, 0))
    return pl.pallas_call(
        functools.partial(_in_proj_kernel, layer=layer),
        grid=(n // t,),
        in_specs=[row(D), _const_spec(g.shape), _layer_spec((D, IN_W), layer), _const_spec(sg.shape)],
        out_specs=[row(POOL_W), row(SGU_W), row(SGU_W), row(FNET_W)],
        out_shape=[jax.ShapeDtypeStruct((n, POOL_W), BF16), jax.ShapeDtypeStruct((n, SGU_W), BF16),
                   jax.ShapeDtypeStruct((n, SGU_W), BF16), jax.ShapeDtypeStruct((n, FNET_W), BF16)],
        scratch_shapes=[pltpu.VMEM((D, IN_W), BF16)],
        compiler_params=_params(1),
        name="in_proj",
    )(x2, g, w, sg)


def _fft_a_kernel(ka_ref, x_ref, o_ref):
    rt = ROW_TILE
    c = x_ref.shape[-1]
    x = x_ref[0].reshape(FFT_N1 * rt, c)
    a = _dot(ka_ref[...], x).astype(BF16)
    o_ref[0] = a.reshape(2, FFT_N1, rt, c)


def _fft_stage_a(x4, ka):
    b, n1, n2, c = x4.shape
    rt = ROW_TILE
    return pl.pallas_call(
        _fft_a_kernel,
        grid=(b, n2 // rt),
        in_specs=[_const_spec((2 * n1 * rt, n1 * rt)),
                  pl.BlockSpec((1, n1, rt, c), lambda i, g: (i, 0, g, 0))],
        out_specs=pl.BlockSpec((1, 2, n1, rt, c), lambda i, g: (i, 0, 0, g, 0)),
        out_shape=jax.ShapeDtypeStruct((b, 2, n1, n2, c), BF16),
        compiler_params=_params(2),
        name="fft_stage_a",
    )(ka, x4)


def _fft_b_kernel(fb_ref, swap_ref, a_ref, fr_ref, fi_ref, y_ref):
    rt = ROW_TILE
    c = a_ref.shape[-1]
    for j in range(rt):
        d = jnp.concatenate([a_ref[0, 0, j], a_ref[0, 1, j]], axis=0)
        y_ref[j] = _dot(fb_ref[j], d).astype(BF16)
    for t in range(FFT_N2 // rt):
        re = y_ref[:, t * rt:(t + 1) * rt, :].reshape(rt * rt, c)
        im = y_ref[:, FFT_N2 + t * rt:FFT_N2 + (t + 1) * rt, :].reshape(rt * rt, c)
        q = _dot(swap_ref[...], jnp.concatenate([re, im], axis=1)).astype(BF16)
        fr_ref[0, t * rt:(t + 1) * rt] = q[:, :c].reshape(rt, rt, c)
        fi_ref[0, t * rt:(t + 1) * rt] = q[:, c:].reshape(rt, rt, c)


def _fft_stage_b(a5, fb, swap):
    b, _, n1, n2, c = a5.shape
    rt = ROW_TILE
    out = pl.BlockSpec((1, n2, rt, c), lambda g, i: (i, 0, g, 0))
    return pl.pallas_call(
        _fft_b_kernel,
        grid=(n1 // rt, b),
        in_specs=[pl.BlockSpec((rt, 2 * n2, 2 * n2), lambda g, i: (g, 0, 0)),
                  _const_spec((rt * rt, rt * rt)),
                  pl.BlockSpec((1, 2, rt, n2, c), lambda g, i: (i, 0, g, 0, 0))],
        out_specs=[out, out],
        out_shape=[jax.ShapeDtypeStruct((b, n2, n1, c), BF16)] * 2,
        scratch_shapes=[pltpu.VMEM((rt, 2 * n2, c), BF16)],
        compiler_params=_params(2),
        name="fft_stage_b",
    )(fb, swap, a5)


def _fold_kernel(dftc_ref, fw_ref, wo_ref, o_ref):
    kd = POOL_W + SGU_W
    o_ref[0:kd, :] = wo_ref[0:kd, :].astype(BF16)
    t = _dot(dftc_ref[...], fw_ref[...]).astype(BF16)
    o_ref[kd:, :] = _dot(t, wo_ref[kd:, :].astype(BF16)).astype(BF16)


def _fold(dftc, fw_bd, w_out):
    depth = fw_bd.shape[0]
    return pl.pallas_call(
        _fold_kernel,
        grid=(depth,),
        in_specs=[_const_spec((2 * FNET_W, FNET_W)),
                  pl.BlockSpec((None, FNET_W, FNET_W), lambda l: (l, 0, 0)),
                  pl.BlockSpec((None, D, D), lambda l: (l, 0, 0))],
        out_specs=pl.BlockSpec((None, MIX_K, D), lambda l: (l, 0, 0)),
        out_shape=jax.ShapeDtypeStruct((depth, MIX_K, D), BF16),
        compiler_params=_params(1),
        name="fold_fourier",
    )(dftc, fw_bd, w_out)


def _mix_out_kernel(x_ref, za_ref, zp_ref, zn_ref, zu_ref, zv_ref, fr_ref, fi_ref,
                    pw_ref, ps_ref, sw_ref, sb_ref, wc_ref, g_ref, edge_ref, o_ref, ext_ref, ym_ref,
                    *, seq, layer):
    pool_scale, g_post = _row(ps_ref, layer), _row(g_ref, layer)
    t = MIX_TILE
    hl = POOL_HALO
    rows = t // MIX_SUB
    ext_rows = rows + 2 * hl
    j = pl.program_id(1)
    nj = seq // t
    ext_ref[0:hl] = jnp.where(j > 0, zp_ref[...].astype(F32), 0.0)
    ext_ref[hl:hl + t] = za_ref[...].astype(F32)
    ext_ref[hl + t:2 * hl + t] = jnp.where(j < nj - 1, zn_ref[...].astype(F32), 0.0)

    lane = lax.broadcasted_iota(jnp.int32, (1, 128), 1)
    low = lane < 64
    inv_win = jnp.concatenate([jnp.where(low, 1.0 / 2, 1.0 / 4), jnp.where(low, 1.0 / 8, 1.0 / 16)], axis=1)
    first_scale = jnp.where(j == 0, edge_ref[0], inv_win)
    last_scale = jnp.where(j == nj - 1, edge_ref[1], inv_win)
    nch = rows // SGU_CHUNK
    hdim = SGU_W // SGU_HEADS

    def down(a, k):
        return pltpu.roll(a, k, 0)

    def up(a, k):
        return pltpu.roll(a, ext_rows - k, 0)

    for sub in range(MIX_SUB):
        r0 = sub * rows
        r = slice(r0, r0 + rows)
        e = ext_ref[r0:r0 + ext_rows, :]
        c2 = e + down(e, 1)
        c4 = c2 + down(c2, 2)
        c4b = c4[:, 128:]
        c8 = c4b + down(c4b, 4)
        c16 = c8 + down(c8, 8)
        s0 = jnp.where(low, c2[:, :128], up(c4[:, :128], 1))
        s1 = jnp.where(low, up(c8, 3), up(c16, 7))
        ws = jnp.concatenate([s0, s1], axis=1)[hl:hl + rows]
        scale_head = first_scale if sub == 0 else inv_win
        scale_tail = last_scale if sub == MIX_SUB - 1 else inv_win
        pooled = jnp.concatenate([ws[:hl] * scale_head, ws[hl:rows - hl] * inv_win,
                                  ws[rows - hl:] * scale_tail], axis=0)
        diff = (pooled - e[hl:hl + rows]).astype(BF16)
        ya = _dot(diff, pw_ref[...]) * pool_scale
        ym_ref[r, 0:POOL_W] = ya.astype(BF16)

        for h in range(SGU_HEADS):
            cols = slice(h * hdim, (h + 1) * hdim)
            vh = jnp.concatenate(
                [zv_ref[r0 + c * SGU_CHUNK:r0 + (c + 1) * SGU_CHUNK, cols] for c in range(nch)], axis=1)
            mh = _dot(sw_ref[h], vh)
            for c in range(nch):
                rc = slice(r0 + c * SGU_CHUNK, r0 + (c + 1) * SGU_CHUNK)
                mixed = mh[:, c * hdim:(c + 1) * hdim] + sb_ref[h]
                ym_ref[rc, POOL_W + h * hdim:POOL_W + (h + 1) * hdim] = (
                    zu_ref[rc, cols].astype(F32) * mixed).astype(BF16)

        ym_ref[r, POOL_W + SGU_W:POOL_W + SGU_W + FNET_W] = fr_ref[r, :]
        ym_ref[r, POOL_W + SGU_W + FNET_W:] = fi_ref[r, :]
        y = _dot(ym_ref[r, :], wc_ref[...])
        o_ref[r, :] = x_ref[r, :] + _rms(y, g_post)


def _mix_out(x2, za, zu, zv, fr, fi, pw_bd, ps, sw, sb_full, wcat, g, edge, layer, *, batch, seq):
    n = x2.shape[0]
    t = MIX_TILE
    hl = POOL_HALO
    nj = seq // t
    row = lambda w_: pl.BlockSpec((t, w_), lambda b, j: (b * nj + j, 0))
    prev = pl.BlockSpec((hl, POOL_W), lambda b, j: (jnp.maximum((b * nj + j) * (t // hl) - 1, 0), 0))
    nxt = pl.BlockSpec((hl, POOL_W), lambda b, j: (jnp.minimum((b * nj + j + 1) * (t // hl), n // hl - 1), 0))
    return pl.pallas_call(
        functools.partial(_mix_out_kernel, seq=seq, layer=layer),
        grid=(batch, nj),
        in_specs=[row(D), row(POOL_W), prev, nxt, row(SGU_W), row(SGU_W), row(FNET_W), row(FNET_W),
                  _layer_spec((POOL_W, POOL_W), layer), _const_spec(ps.shape),
                  _layer_spec((SGU_HEADS, SGU_CHUNK, SGU_CHUNK), layer),
                  _layer_spec((SGU_HEADS, SGU_CHUNK, SGU_CHUNK), layer),
                  _layer_spec((MIX_K, D), layer), _const_spec(g.shape),
                  _const_spec((2, hl, POOL_W))],
        out_specs=row(D),
        out_shape=jax.ShapeDtypeStruct((n, D), F32),
        scratch_shapes=[pltpu.VMEM((t + 2 * hl, POOL_W), F32), pltpu.VMEM((t, MIX_K), BF16)],
        compiler_params=_params(2),
        name="mix_out",
    )(x2, za, za, za, zu, zv, fr, fi, pw_bd, ps, sw, sb_full, wcat, g, edge)


def _kv_kernel(m_ref, g_ref, wk_ref, wv_ref, k_ref, v_ref, wkb_ref, wvb_ref):
    @pl.when(pl.program_id(1) == 0)
    def _():
        wkb_ref[...] = wk_ref[...].astype(BF16)
        wvb_ref[...] = wv_ref[...].astype(BF16)

    g = g_ref[pl.ds(pl.program_id(0), 1), :]
    m = _rms(m_ref[0], g).astype(BF16)
    k_ref[...] = _dot(m, wkb_ref[...]).astype(BF16)
    v_ref[...] = _dot(m, wvb_ref[...]).astype(BF16)


def _kv_proj(mem, g, wk, wv):
    b = mem.shape[0]
    depth = wk.shape[0]
    weight = pl.BlockSpec((None, D, D), lambda l, i: (l, 0, 0))
    out = pl.BlockSpec((None, None, MEM_LEN, D), lambda l, i: (l, i, 0, 0))
    return pl.pallas_call(
        _kv_kernel,
        grid=(depth, b),
        in_specs=[pl.BlockSpec((1, MEM_LEN, D), lambda l, i: (i, 0, 0)), _const_spec(g.shape), weight, weight],
        out_specs=[out, out],
        out_shape=[jax.ShapeDtypeStruct((depth, b, MEM_LEN, D), BF16)] * 2,
        scratch_shapes=[pltpu.VMEM((D, D), BF16)] * 2,
        compiler_params=_params(2),
        name="kv_proj",
    )(mem, g, wk, wv)


def _attn_rows(x, k_ref, v_ref, g_pre, wq_ref, wo_ref, g_post, oh_ref, r):
    h = _rms(x, g_pre).astype(BF16)
    q = (_dot(h, wq_ref[...]) * (XA_HD ** -0.5)).astype(BF16)
    for hd in range(XA_HEADS):
        sl = slice(hd * XA_HD, (hd + 1) * XA_HD)
        s = lax.dot_general(q[:, sl], k_ref[:, sl], (((1,), (1,)), ((), ())),
                            preferred_element_type=F32)
        e = jnp.exp(s - jnp.max(s, axis=-1, keepdims=True))
        p = (e / jnp.sum(e, axis=-1, keepdims=True)).astype(BF16)
        oh_ref[r, sl] = _dot(p, v_ref[:, sl]).astype(BF16)
    y = _dot(oh_ref[r, :], wo_ref[...])
    return x + _rms(y, g_post)


def _ffn_rows(x, g_pre, wg_ref, wu_ref, wd_ref, g_post):
    h = _rms(x, g_pre).astype(BF16)
    y = None
    off = 0
    for ch in FFN_CHUNKS:
        g = _dot(h, wg_ref[:, off:off + ch])
        u = _dot(h, wu_ref[:, off:off + ch])
        a = (g * jax.nn.sigmoid(g) * u).astype(BF16)
        part = _dot(a, wd_ref[off:off + ch, :])
        y = part if y is None else y + part
        off += ch
    return x + _rms(y, g_post)


def _attn_kernel(x_ref, k_ref, v_ref, gpre_ref, wq_ref, wo_ref, gpost_ref, o_ref, oh_ref, wqb_ref, wob_ref,
                 *, layer):
    @pl.when((pl.program_id(0) == 0) & (pl.program_id(1) == 0))
    def _():
        wqb_ref[...] = wq_ref[...].astype(BF16)
        wob_ref[...] = wo_ref[...].astype(BF16)

    g_pre, g_post = _row(gpre_ref, layer), _row(gpost_ref, layer)
    rows = x_ref.shape[0] // ATT_SUB
    for sub in range(ATT_SUB):
        r = slice(sub * rows, (sub + 1) * rows)
        o_ref[r, :] = _attn_rows(x_ref[r, :], k_ref, v_ref, g_pre, wqb_ref, wob_ref, g_post, oh_ref, r)


def _attention(x2, k, v, gpre, wq, wo, gpost, layer, *, batch, seq):
    n = x2.shape[0]
    t = ATT_TILE
    nj = seq // t
    row = pl.BlockSpec((t, D), lambda b, j: (b * nj + j, 0))
    kv = pl.BlockSpec((None, None, MEM_LEN, D), lambda b, j: (layer, b, 0, 0))
    return pl.pallas_call(
        functools.partial(_attn_kernel, layer=layer),
        grid=(batch, nj),
        in_specs=[row, kv, kv, _const_spec(gpre.shape), _layer_spec((D, D), layer),
                  _layer_spec((D, D), layer), _const_spec(gpost.shape)],
        out_specs=row,
        out_shape=jax.ShapeDtypeStruct((n, D), F32),
        scratch_shapes=[pltpu.VMEM((t, D), BF16), pltpu.VMEM((D, D), BF16), pltpu.VMEM((D, D), BF16)],
        compiler_params=_params(2),
        name="mem_attention",
    )(x2, k, v, gpre, wq, wo, gpost)


def _ffn_kernel(x_ref, gpre_ref, wg_ref, wu_ref, wd_ref, gpost_ref, o_ref, *, layer):
    g_pre, g_post = _row(gpre_ref, layer), _row(gpost_ref, layer)
    rows = x_ref.shape[0] // FFN_SUB
    for sub in range(FFN_SUB):
        r = slice(sub * rows, (sub + 1) * rows)
        o_ref[r, :] = _ffn_rows(x_ref[r, :], g_pre, wg_ref, wu_ref, wd_ref, g_post)


def _ffn(x2, gpre, wg, wu, wd, gpost, layer):
    n = x2.shape[0]
    t = FFN_TILE
    row = pl.BlockSpec((t, D), lambda i: (i, 0))
    return pl.pallas_call(
        functools.partial(_ffn_kernel, layer=layer),
        grid=(n // t,),
        in_specs=[row, _const_spec(gpre.shape), _layer_spec((D, FFN_H), layer),
                  _layer_spec((D, FFN_H), layer), _layer_spec((FFN_H, D), layer),
                  _const_spec(gpost.shape)],
        out_specs=row,
        out_shape=jax.ShapeDtypeStruct((n, D), F32),
        compiler_params=_params(1),
        name="swiglu",
    )(x2, gpre, wg, wu, wd, gpost)


def kernel(x, mem, ln_mix_pre, w_in, pool_w, pool_scale, sgu_norm, sgu_w, sgu_b, fnet_w, w_out,
           ln_mix_post, ln_xa_pre, ln_mem, xa_wq, xa_wk, xa_wv, xa_wo, ln_xa_post, ln_ffn_pre,
           ffn_wg, ffn_wu, ffn_wd, ln_ffn_post):
    batch, seq, d = x.shape
    depth = w_in.shape[0]
    assert d == D and sum(FFN_CHUNKS) == FFN_H
    assert all(seq % t == 0 for t in (IN_TILE, MIX_TILE, ATT_TILE, FFN_TILE))
    assert (MIX_TILE // MIX_SUB) % SGU_CHUNK == 0
    ka, fb, dftc, swap = (jnp.asarray(t).astype(BF16) for t in _fft_tables(seq))
    edge = jnp.asarray(_pool_edge_scales(seq))
    n = batch * seq
    x2 = x.reshape(n, d)
    bf = lambda a: a.astype(BF16)

    wg_b, wu_b, wd_b = bf(ffn_wg), bf(ffn_wu), bf(ffn_wd)
    pw_bd = bf(_block_diag(pool_w))
    sw_b = bf(sgu_w)
    sb_full = jnp.broadcast_to(sgu_b[:, :, :, None], sgu_b.shape + (SGU_CHUNK,))
    wcat = _fold(dftc, bf(_block_diag(fnet_w)), w_out)
    k_all, v_all = _kv_proj(mem, ln_mem, xa_wk, xa_wv)

    for l in range(depth):
        za, zu, zv, zc = _in_proj(x2, ln_mix_pre, w_in, sgu_norm, l)
        a5 = _fft_stage_a(zc.reshape(batch, FFT_N1, FFT_N2, FNET_W), ka)
        fr, fi = _fft_stage_b(a5, fb, swap)
        x2 = _mix_out(x2, za, zu, zv, fr.reshape(n, FNET_W), fi.reshape(n, FNET_W), pw_bd, pool_scale, sw_b,
                      sb_full, wcat, ln_mix_post, edge, l, batch=batch, seq=seq)
        x2 = _attention(x2, k_all, v_all, ln_xa_pre, xa_wq, xa_wo, ln_xa_post, l, batch=batch, seq=seq)
        x2 = _ffn(x2, ln_ffn_pre, wg_b, wu_b, wd_b, ln_ffn_post, l)
    return x2.reshape(batch, seq, d)
```

```python
import functools

import numpy as np
import jax
import jax.numpy as jnp
from jax import lax
from jax.experimental import pallas as pl
from jax.experimental.pallas import tpu as pltpu

F32 = jnp.float32
BF16 = jnp.bfloat16

D = 1024
POOL_W = 256
POOL_WINDOWS = (2, 4, 8, 16)
SGU_W = 512
FNET_W = 256
IN_W = POOL_W + 2 * SGU_W + FNET_W
MIX_K = POOL_W + SGU_W + 2 * FNET_W
SGU_CHUNK = 128
SGU_HEADS = 4
MEM_LEN = 256
XA_HEADS = 4
XA_HD = D // XA_HEADS
FFN_H = 2816
RMS_EPS = 1e-6
LN_EPS = 1e-5

FFT_N1 = 64
FFT_N2 = 128
ROW_TILE = 16
POOL_HALO = 16
IN_TILE, IN_SUB = 2048, 4
MIX_TILE, MIX_SUB = 1024, 2
ATT_TILE, ATT_SUB = 2048, 4
FFN_TILE, FFN_SUB = 1024, 2
FFN_CHUNKS = (768, 768, 768, 512)
CAST_ROWS = 256
VMEM_LIMIT = 56 * 1024 * 1024


def _rms_scale(xf):
    return lax.rsqrt(jnp.mean(xf * xf, axis=-1, keepdims=True) + RMS_EPS)


def _rms(xf, g):
    return xf * _rms_scale(xf) * g


def _dot(a, b):
    return jnp.dot(a, b, preferred_element_type=F32)


def _params(n_axes):
    return pltpu.CompilerParams(
        dimension_semantics=("arbitrary",) * n_axes, vmem_limit_bytes=VMEM_LIMIT)


def _const_spec(shape):
    nd = len(shape)
    return pl.BlockSpec(shape, lambda *_: (0,) * nd, pipeline_mode=pl.Buffered(1))


def _layer_spec(shape, layer):
    nd = len(shape)
    return pl.BlockSpec((None,) + tuple(shape), lambda *_: (layer,) + (0,) * nd,
                        pipeline_mode=pl.Buffered(1))


def _row(ref, layer):
    return ref[layer:layer + 1, :]


@functools.lru_cache(maxsize=None)
def _fft_tables(seq):
    assert seq == FFT_N1 * FFT_N2
    two_pi = 2.0 * np.pi
    k1 = np.arange(FFT_N1)
    ang = two_pi * ((k1[:, None] * k1[None, :]) % FFT_N1) / FFT_N1
    fa = np.concatenate([np.cos(ang), -np.sin(ang)], axis=0) / np.sqrt(FFT_N1)
    k2 = np.arange(FFT_N2)
    k = k1[:, None, None] + FFT_N1 * k2[None, :, None]
    ph = two_pi * ((k * k2[None, None, :]) % seq) / seq
    c, s = np.cos(ph), np.sin(ph)
    fb = np.concatenate([np.concatenate([c, s], axis=2),
                         np.concatenate([-s, c], axis=2)], axis=1) / np.sqrt(FFT_N2)
    hd = FNET_W // 4
    cc = np.arange(hd)
    ang = two_pi * ((cc[:, None] * cc[None, :]) % hd) / hd
    eye = np.eye(4)
    dftc = np.concatenate([np.kron(eye, np.cos(ang)), np.kron(eye, np.sin(ang))], axis=0) / np.sqrt(hd)
    ka = np.kron(fa, np.eye(ROW_TILE))
    o = np.arange(ROW_TILE * ROW_TILE)
    swap = np.eye(ROW_TILE * ROW_TILE)[(o % ROW_TILE) * ROW_TILE + o // ROW_TILE]
    return tuple(np.asarray(t, dtype=np.float32) for t in (ka, fb, dftc, swap))


@functools.lru_cache(maxsize=None)
def _pool_edge_scales(seq):
    hl = POOL_HALO
    pos = np.concatenate([np.arange(hl), np.arange(seq - hl, seq)])[:, None]
    win = np.repeat(np.asarray(POOL_WINDOWS), POOL_W // len(POOL_WINDOWS))[None, :]
    left = win // 2
    right = win - 1 - left
    cnt = np.minimum(pos + right, seq - 1) - np.maximum(pos - left, 0) + 1
    return (1.0 / cnt).astype(np.float32).reshape(2, hl, POOL_W)


def _block_diag(w):
    l, g, c, _ = w.shape
    eye = jnp.eye(g, dtype=w.dtype)
    return (eye[None, :, None, :, None] * w[:, :, :, None, :]).reshape(l, g * c, g * c)


def _in_proj_kernel(x_ref, g_ref, w_ref, sg_ref, za_ref, zu_ref, zv_ref, zc_ref, wb_ref, *, layer):
    @pl.when(pl.program_id(0) == 0)
    def _():
        wb_ref[...] = (w_ref[...] * g_ref[...]).astype(BF16)

    sg = _row(sg_ref, layer)
    rows = x_ref.shape[0] // IN_SUB
    for sub in range(IN_SUB):
        r = slice(sub * rows, (sub + 1) * rows)
        x = x_ref[r, :]
        z = _dot(x.astype(BF16), wb_ref[...]) * _rms_scale(x)
        za_ref[r, :] = z[:, :POOL_W].astype(BF16)
        zu_ref[r, :] = z[:, POOL_W:POOL_W + SGU_W].astype(BF16)
        v = z[:, POOL_W + SGU_W:POOL_W + 2 * SGU_W]
        vc = v - jnp.mean(v, axis=-1, keepdims=True)
        vn = vc * lax.rsqrt(jnp.mean(vc * vc, axis=-1, keepdims=True) + LN_EPS) * sg
        zv_ref[r, :] = vn.astype(BF16)
        zc_ref[r, :] = z[:, POOL_W + 2 * SGU_W:].astype(BF16)


def _in_proj(x2, g, w, sg, layer):
    n = x2.shape[0]
    t = IN_TILE
    row = lambda w_: pl.BlockSpec((t, w_), lambda i: (i, 0))
    return pl.pallas_call(
        functools.partial(_in_proj_kernel, layer=layer),
        grid=(n // t,),
        in_specs=[row(D), _layer_spec((D, 1), layer), _layer_spec((D, IN_W), layer), _const_spec(sg.shape)],
        out_specs=[row(POOL_W), row(SGU_W), row(SGU_W), row(FNET_W)],
        out_shape=[jax.ShapeDtypeStruct((n, POOL_W), BF16), jax.ShapeDtypeStruct((n, SGU_W), BF16),
                   jax.ShapeDtypeStruct((n, SGU_W), BF16), jax.ShapeDtypeStruct((n, FNET_W), BF16)],
        scratch_shapes=[pltpu.VMEM((D, IN_W), BF16)],
        compiler_params=_params(1),
        name="in_proj",
    )(x2, g, w, sg)


def _fft_a_kernel(ka_ref, x_ref, o_ref):
    rt = ROW_TILE
    c = x_ref.shape[-1]
    x = x_ref[0].reshape(FFT_N1 * rt, c)
    a = _dot(ka_ref[...], x).astype(BF16)
    o_ref[0] = a.reshape(2, FFT_N1, rt, c)


def _fft_stage_a(x4, ka):
    b, n1, n2, c = x4.shape
    rt = ROW_TILE
    return pl.pallas_call(
        _fft_a_kernel,
        grid=(b, n2 // rt),
        in_specs=[_const_spec((2 * n1 * rt, n1 * rt)),
                  pl.BlockSpec((1, n1, rt, c), lambda i, g: (i, 0, g, 0))],
        out_specs=pl.BlockSpec((1, 2, n1, rt, c), lambda i, g: (i, 0, 0, g, 0)),
        out_shape=jax.ShapeDtypeStruct((b, 2, n1, n2, c), BF16),
        compiler_params=_params(2),
        name="fft_stage_a",
    )(ka, x4)


def _fft_b_kernel(fb_ref, swap_ref, a_ref, fr_ref, fi_ref, y_ref):
    rt = ROW_TILE
    c = a_ref.shape[-1]
    for j in range(rt):
        d = jnp.concatenate([a_ref[0, 0, j], a_ref[0, 1, j]], axis=0)
        y_ref[j] = _dot(fb_ref[j], d).astype(BF16)
    for t in range(FFT_N2 // rt):
        re = y_ref[:, t * rt:(t + 1) * rt, :].reshape(rt * rt, c)
        im = y_ref[:, FFT_N2 + t * rt:FFT_N2 + (t + 1) * rt, :].reshape(rt * rt, c)
        q = _dot(swap_ref[...], jnp.concatenate([re, im], axis=1)).astype(BF16)
        fr_ref[0, t * rt:(t + 1) * rt] = q[:, :c].reshape(rt, rt, c)
        fi_ref[0, t * rt:(t + 1) * rt] = q[:, c:].reshape(rt, rt, c)


def _fft_stage_b(a5, fb, swap):
    b, _, n1, n2, c = a5.shape
    rt = ROW_TILE
    out = pl.BlockSpec((1, n2, rt, c), lambda g, i: (i, 0, g, 0))
    return pl.pallas_call(
        _fft_b_kernel,
        grid=(n1 // rt, b),
        in_specs=[pl.BlockSpec((rt, 2 * n2, 2 * n2), lambda g, i: (g, 0, 0)),
                  _const_spec((rt * rt, rt * rt)),
                  pl.BlockSpec((1, 2, rt, n2, c), lambda g, i: (i, 0, g, 0, 0))],
        out_specs=[out, out],
        out_shape=[jax.ShapeDtypeStruct((b, n2, n1, c), BF16)] * 2,
        scratch_shapes=[pltpu.VMEM((rt, 2 * n2, c), BF16)],
        compiler_params=_params(2),
        name="fft_stage_b",
    )(fb, swap, a5)


def _fold_kernel(dftc_ref, fw_ref, wo_ref, o_ref):
    kd = POOL_W + SGU_W
    o_ref[0:kd, :] = wo_ref[0:kd, :].astype(BF16)
    t = _dot(dftc_ref[...], fw_ref[...]).astype(BF16)
    o_ref[kd:, :] = _dot(t, wo_ref[kd:, :].astype(BF16)).astype(BF16)


def _fold(dftc, fw_bd, w_out):
    depth = fw_bd.shape[0]
    return pl.pallas_call(
        _fold_kernel,
        grid=(depth,),
        in_specs=[_const_spec((2 * FNET_W, FNET_W)),
                  pl.BlockSpec((None, FNET_W, FNET_W), lambda l: (l, 0, 0)),
                  pl.BlockSpec((None, D, D), lambda l: (l, 0, 0))],
        out_specs=pl.BlockSpec((None, MIX_K, D), lambda l: (l, 0, 0)),
        out_shape=jax.ShapeDtypeStruct((depth, MIX_K, D), BF16),
        compiler_params=_params(1),
        name="fold_fourier",
    )(dftc, fw_bd, w_out)


def _mix_out_kernel(x_ref, za_ref, zp_ref, zn_ref, zu_ref, zv_ref, fr_ref, fi_ref,
                    pw_ref, ps_ref, sw_ref, sb_ref, wc_ref, g_ref, edge_ref, o_ref, ext_ref, ym_ref,
                    *, seq, layer):
    pool_scale, g_post = _row(ps_ref, layer), _row(g_ref, layer)
    t = MIX_TILE
    hl = POOL_HALO
    rows = t // MIX_SUB
    ext_rows = rows + 2 * hl
    j = pl.program_id(1)
    nj = seq // t
    ext_ref[0:hl] = jnp.where(j > 0, zp_ref[...].astype(F32), 0.0)
    ext_ref[hl:hl + t] = za_ref[...].astype(F32)
    ext_ref[hl + t:2 * hl + t] = jnp.where(j < nj - 1, zn_ref[...].astype(F32), 0.0)

    lane = lax.broadcasted_iota(jnp.int32, (1, 128), 1)
    low = lane < 64
    inv_win = jnp.concatenate([jnp.where(low, 1.0 / 2, 1.0 / 4), jnp.where(low, 1.0 / 8, 1.0 / 16)], axis=1)
    first_scale = jnp.where(j == 0, edge_ref[0], inv_win)
    last_scale = jnp.where(j == nj - 1, edge_ref[1], inv_win)
    nch = rows // SGU_CHUNK
    hdim = SGU_W // SGU_HEADS

    def down(a, k):
        return pltpu.roll(a, k, 0)

    def up(a, k):
        return pltpu.roll(a, ext_rows - k, 0)

    for sub in range(MIX_SUB):
        r0 = sub * rows
        r = slice(r0, r0 + rows)
        e = ext_ref[r0:r0 + ext_rows, :]
        c2 = e + down(e, 1)
        c4 = c2 + down(c2, 2)
        c4b = c4[:, 128:]
        c8 = c4b + down(c4b, 4)
        c16 = c8 + down(c8, 8)
        s0 = jnp.where(low, c2[:, :128], up(c4[:, :128], 1))
        s1 = jnp.where(low, up(c8, 3), up(c16, 7))
        ws = jnp.concatenate([s0, s1], axis=1)[hl:hl + rows]
        scale_head = first_scale if sub == 0 else inv_win
        scale_tail = last_scale if sub == MIX_SUB - 1 else inv_win
        pooled = jnp.concatenate([ws[:hl] * scale_head, ws[hl:rows - hl] * inv_win,
                                  ws[rows - hl:] * scale_tail], axis=0)
        diff = (pooled - e[hl:hl + rows]).astype(BF16)
        ya = _dot(diff, pw_ref[...]) * pool_scale
        ym_ref[r, 0:POOL_W] = ya.astype(BF16)

        for h in range(SGU_HEADS):
            cols = slice(h * hdim, (h + 1) * hdim)
            vh = jnp.concatenate(
                [zv_ref[r0 + c * SGU_CHUNK:r0 + (c + 1) * SGU_CHUNK, cols] for c in range(nch)], axis=1)
            mh = _dot(sw_ref[h], vh)
            for c in range(nch):
                rc = slice(r0 + c * SGU_CHUNK, r0 + (c + 1) * SGU_CHUNK)
                mixed = mh[:, c * hdim:(c + 1) * hdim] + sb_ref[h]
                ym_ref[rc, POOL_W + h * hdim:POOL_W + (h + 1) * hdim] = (
                    zu_ref[rc, cols].astype(F32) * mixed).astype(BF16)

        ym_ref[r, POOL_W + SGU_W:POOL_W + SGU_W + FNET_W] = fr_ref[r, :]
        ym_ref[r, POOL_W + SGU_W + FNET_W:] = fi_ref[r, :]
        y = _dot(ym_ref[r, :], wc_ref[...])
        o_ref[r, :] = x_ref[r, :] + _rms(y, g_post)


def _mix_out(x2, za, zu, zv, fr, fi, pw_bd, ps, sw, sb_full, wcat, g, edge, layer, *, batch, seq):
    n = x2.shape[0]
    t = MIX_TILE
    hl = POOL_HALO
    nj = seq // t
    row = lambda w_: pl.BlockSpec((t, w_), lambda b, j: (b * nj + j, 0))
    prev = pl.BlockSpec((hl, POOL_W), lambda b, j: (jnp.maximum((b * nj + j) * (t // hl) - 1, 0), 0))
    nxt = pl.BlockSpec((hl, POOL_W), lambda b, j: (jnp.minimum((b * nj + j + 1) * (t // hl), n // hl - 1), 0))
    return pl.pallas_call(
        functools.partial(_mix_out_kernel, seq=seq, layer=layer),
        grid=(batch, nj),
        in_specs=[row(D), row(POOL_W), prev, nxt, row(SGU_W), row(SGU_W), row(FNET_W), row(FNET_W),
                  _layer_spec((POOL_W, POOL_W), layer), _const_spec(ps.shape),
                  _layer_spec((SGU_HEADS, SGU_CHUNK, SGU_CHUNK), layer),
                  _layer_spec((SGU_HEADS, SGU_CHUNK, SGU_CHUNK), layer),
                  _layer_spec((MIX_K, D), layer), _const_spec(g.shape),
                  _const_spec((2, hl, POOL_W))],
        out_specs=row(D),
        out_shape=jax.ShapeDtypeStruct((n, D), F32),
        scratch_shapes=[pltpu.VMEM((t + 2 * hl, POOL_W), F32), pltpu.VMEM((t, MIX_K), BF16)],
        compiler_params=_params(2),
        name="mix_out",
    )(x2, za, za, za, zu, zv, fr, fi, pw_bd, ps, sw, sb_full, wcat, g, edge)


def _kv_kernel(m_ref, g_ref, wk_ref, wv_ref, k_ref, v_ref, wkb_ref, wvb_ref):
    @pl.when(pl.program_id(1) == 0)
    def _():
        wkb_ref[...] = wk_ref[...].astype(BF16)
        wvb_ref[...] = wv_ref[...].astype(BF16)

    g = g_ref[pl.ds(pl.program_id(0), 1), :]
    m = _rms(m_ref[0], g).astype(BF16)
    k_ref[...] = _dot(m, wkb_ref[...]).astype(BF16)
    v_ref[...] = _dot(m, wvb_ref[...]).astype(BF16)


def _kv_proj(mem, g, wk, wv):
    b = mem.shape[0]
    depth = wk.shape[0]
    weight = pl.BlockSpec((None, D, D), lambda l, i: (l, 0, 0))
    out = pl.BlockSpec((None, None, MEM_LEN, D), lambda l, i: (l, i, 0, 0))
    return pl.pallas_call(
        _kv_kernel,
        grid=(depth, b),
        in_specs=[pl.BlockSpec((1, MEM_LEN, D), lambda l, i: (i, 0, 0)), _const_spec(g.shape), weight, weight],
        out_specs=[out, out],
        out_shape=[jax.ShapeDtypeStruct((depth, b, MEM_LEN, D), BF16)] * 2,
        scratch_shapes=[pltpu.VMEM((D, D), BF16)] * 2,
        compiler_params=_params(2),
        name="kv_proj",
    )(mem, g, wk, wv)


def _attn_rows(x, k_ref, v_ref, wq_ref, wo_ref, g_post, oh_ref, r):
    q = (_dot(x.astype(BF16), wq_ref[...]) * (_rms_scale(x) * XA_HD ** -0.5)).astype(BF16)
    for hd in range(XA_HEADS):
        sl = slice(hd * XA_HD, (hd + 1) * XA_HD)
        s = lax.dot_general(q[:, sl], k_ref[:, sl], (((1,), (1,)), ((), ())),
                            preferred_element_type=F32)
        e = jnp.exp(s - jnp.max(s, axis=-1, keepdims=True))
        o = _dot(e.astype(BF16), v_ref[:, sl]) / jnp.sum(e, axis=-1, keepdims=True)
        oh_ref[r, sl] = o.astype(BF16)
    y = _dot(oh_ref[r, :], wo_ref[...])
    return x + _rms(y, g_post)


def _ffn_rows(x, wg_ref, wu_ref, wd_ref, g_post):
    h = x.astype(BF16)
    scale = _rms_scale(x)
    y = None
    off = 0
    for ch in FFN_CHUNKS:
        g = _dot(h, wg_ref[:, off:off + ch]) * scale
        u = _dot(h, wu_ref[:, off:off + ch]) * scale
        a = (g * jax.nn.sigmoid(g) * u).astype(BF16)
        part = _dot(a, wd_ref[off:off + ch, :])
        y = part if y is None else y + part
        off += ch
    return x + _rms(y, g_post)


def _attn_kernel(x_ref, k_ref, v_ref, gpre_ref, wq_ref, wo_ref, gpost_ref, o_ref, oh_ref, wqb_ref, wob_ref,
                 *, layer):
    @pl.when((pl.program_id(0) == 0) & (pl.program_id(1) == 0))
    def _():
        wqb_ref[...] = (wq_ref[...] * gpre_ref[...]).astype(BF16)
        wob_ref[...] = wo_ref[...].astype(BF16)

    g_post = _row(gpost_ref, layer)
    rows = x_ref.shape[0] // ATT_SUB
    for sub in range(ATT_SUB):
        r = slice(sub * rows, (sub + 1) * rows)
        o_ref[r, :] = _attn_rows(x_ref[r, :], k_ref, v_ref, wqb_ref, wob_ref, g_post, oh_ref, r)


def _attention(x2, k, v, gpre, wq, wo, gpost, layer, *, batch, seq):
    n = x2.shape[0]
    t = ATT_TILE
    nj = seq // t
    row = pl.BlockSpec((t, D), lambda b, j: (b * nj + j, 0))
    kv = pl.BlockSpec((None, None, MEM_LEN, D), lambda b, j: (layer, b, 0, 0))
    return pl.pallas_call(
        functools.partial(_attn_kernel, layer=layer),
        grid=(batch, nj),
        in_specs=[row, kv, kv, _layer_spec((D, 1), layer), _layer_spec((D, D), layer),
                  _layer_spec((D, D), layer), _const_spec(gpost.shape)],
        out_specs=row,
        out_shape=jax.ShapeDtypeStruct((n, D), F32),
        scratch_shapes=[pltpu.VMEM((t, D), BF16), pltpu.VMEM((D, D), BF16), pltpu.VMEM((D, D), BF16)],
        compiler_params=_params(2),
        name="mem_attention",
    )(x2, k, v, gpre, wq, wo, gpost)


def _ffn_kernel(x_ref, wg_ref, wu_ref, wd_ref, gpost_ref, o_ref, *, layer):
    g_post = _row(gpost_ref, layer)
    rows = x_ref.shape[0] // FFN_SUB
    for sub in range(FFN_SUB):
        r = slice(sub * rows, (sub + 1) * rows)
        o_ref[r, :] = _ffn_rows(x_ref[r, :], wg_ref, wu_ref, wd_ref, g_post)


def _ffn(x2, wg, wu, wd, gpost, layer):
    n = x2.shape[0]
    t = FFN_TILE
    row = pl.BlockSpec((t, D), lambda i: (i, 0))
    return pl.pallas_call(
        functools.partial(_ffn_kernel, layer=layer),
        grid=(n // t,),
        in_specs=[row, _layer_spec((D, FFN_H), layer), _layer_spec((D, FFN_H), layer),
                  _layer_spec((FFN_H, D), layer), _const_spec(gpost.shape)],
        out_specs=row,
        out_shape=jax.ShapeDtypeStruct((n, D), F32),
        compiler_params=_params(1),
        name="swiglu",
    )(x2, wg, wu, wd, gpost)


def _cast_kernel(w_ref, g_ref, o_ref):
    o_ref[...] = (w_ref[...] * g_ref[...]).astype(BF16)


def _scaled_cast(w, gcol):
    depth, r, c = w.shape
    rb = CAST_ROWS
    assert r % rb == 0
    return pl.pallas_call(
        _cast_kernel,
        grid=(depth, r // rb),
        in_specs=[pl.BlockSpec((None, rb, c), lambda l, i: (l, i, 0)),
                  pl.BlockSpec((None, rb, 1), lambda l, i: (l, i, 0))],
        out_specs=pl.BlockSpec((None, rb, c), lambda l, i: (l, i, 0)),
        out_shape=jax.ShapeDtypeStruct(w.shape, BF16),
        compiler_params=_params(2),
        name="scaled_cast",
    )(w, gcol)


def kernel(x, mem, ln_mix_pre, w_in, pool_w, pool_scale, sgu_norm, sgu_w, sgu_b, fnet_w, w_out,
           ln_mix_post, ln_xa_pre, ln_mem, xa_wq, xa_wk, xa_wv, xa_wo, ln_xa_post, ln_ffn_pre,
           ffn_wg, ffn_wu, ffn_wd, ln_ffn_post):
    batch, seq, d = x.shape
    depth = w_in.shape[0]
    assert d == D and sum(FFN_CHUNKS) == FFN_H
    assert all(seq % t == 0 for t in (IN_TILE, MIX_TILE, ATT_TILE, FFN_TILE))
    assert (MIX_TILE // MIX_SUB) % SGU_CHUNK == 0
    ka, fb, dftc, swap = (jnp.asarray(t).astype(BF16) for t in _fft_tables(seq))
    edge = jnp.asarray(_pool_edge_scales(seq))
    n = batch * seq
    x2 = x.reshape(n, d)
    bf = lambda a: a.astype(BF16)
    col = lambda g: g[:, :, None]

    wg_b = _scaled_cast(ffn_wg, col(ln_ffn_pre))
    wu_b = _scaled_cast(ffn_wu, col(ln_ffn_pre))
    wd_b = bf(ffn_wd)
    pw_bd = bf(_block_diag(pool_w))
    sw_b = bf(sgu_w)
    sb_full = jnp.broadcast_to(sgu_b[:, :, :, None], sgu_b.shape + (SGU_CHUNK,))
    wcat = _fold(dftc, bf(_block_diag(fnet_w)), w_out)
    k_all, v_all = _kv_proj(mem, ln_mem, xa_wk, xa_wv)

    for l in range(depth):
        za, zu, zv, zc = _in_proj(x2, col(ln_mix_pre), w_in, sgu_norm, l)
        a5 = _fft_stage_a(zc.reshape(batch, FFT_N1, FFT_N2, FNET_W), ka)
        fr, fi = _fft_stage_b(a5, fb, swap)
        x2 = _mix_out(x2, za, zu, zv, fr.reshape(n, FNET_W), fi.reshape(n, FNET_W), pw_bd, pool_scale, sw_b,
                      sb_full, wcat, ln_mix_post, edge, l, batch=batch, seq=seq)
        x2 = _attention(x2, k_all, v_all, col(ln_xa_pre), xa_wq, xa_wo, ln_xa_post, l, batch=batch, seq=seq)
        x2 = _ffn(x2, wg_b, wu_b, wd_b, ln_ffn_post, l)
    return x2.reshape(batch, seq, d)
```

```python
import functools

import numpy as np
import jax
import jax.numpy as jnp
from jax import lax
from jax.experimental import pallas as pl
from jax.experimental.pallas import tpu as pltpu

F32 = jnp.float32
BF16 = jnp.bfloat16

D = 1024
POOL_W = 256
POOL_WINDOWS = (2, 4, 8, 16)
SGU_W = 512
FNET_W = 256
IN_W = POOL_W + 2 * SGU_W + FNET_W
MIX_K = POOL_W + SGU_W + 2 * FNET_W
SGU_CHUNK = 128
SGU_HEADS = 4
MEM_LEN = 256
XA_HEADS = 4
XA_HD = D // XA_HEADS
FFN_H = 2816
RMS_EPS = 1e-6
LN_EPS = 1e-5

FFT_N1 = 64
FFT_N2 = 128
FFT_K1_HALF = FFT_N1 // 2 + 1
ROW_TILE = 16
POOL_HALO = 16
IN_TILE, IN_SUB = 2048, 4
MIX_TILE, MIX_SUB = 1024, 2
ATT_TILE, ATT_SUB = 2048, 4
FFN_TILE, FFN_SUB = 1024, 2
FFN_CHUNKS = (768, 768, 768, 512)
CAST_ROWS = 256
VMEM_LIMIT = 56 * 1024 * 1024


def _rms_scale(xf):
    return lax.rsqrt(jnp.mean(xf * xf, axis=-1, keepdims=True) + RMS_EPS)


def _rms(xf, g):
    return xf * _rms_scale(xf) * g


def _dot(a, b):
    return jnp.dot(a, b, preferred_element_type=F32)


def _params(n_axes):
    return pltpu.CompilerParams(
        dimension_semantics=("arbitrary",) * n_axes, vmem_limit_bytes=VMEM_LIMIT)


def _const_spec(shape):
    nd = len(shape)
    return pl.BlockSpec(shape, lambda *_: (0,) * nd, pipeline_mode=pl.Buffered(1))


def _layer_spec(shape, layer):
    nd = len(shape)
    return pl.BlockSpec((None,) + tuple(shape), lambda *_: (layer,) + (0,) * nd,
                        pipeline_mode=pl.Buffered(1))


def _row(ref, layer):
    return ref[layer:layer + 1, :]


@functools.lru_cache(maxsize=None)
def _fft_tables(seq):
    assert seq == FFT_N1 * FFT_N2
    two_pi = 2.0 * np.pi
    k1 = np.arange(FFT_N1)
    ang = two_pi * ((k1[:, None] * k1[None, :]) % FFT_N1) / FFT_N1
    ang = ang[:FFT_N1 // 2 + 1]
    fa = np.concatenate([np.cos(ang), -np.sin(ang)], axis=0) / np.sqrt(FFT_N1)
    k2 = np.arange(FFT_N2)
    k = k1[:, None, None] + FFT_N1 * k2[None, :, None]
    ph = two_pi * ((k * k2[None, None, :]) % seq) / seq
    c, s = np.cos(ph), np.sin(ph)
    sign = np.where(k1 > FFT_N1 // 2, -1.0, 1.0)[:, None, None]
    fb = np.concatenate([np.concatenate([c, s * sign], axis=2),
                         np.concatenate([-s, c * sign], axis=2)], axis=1) / np.sqrt(FFT_N2)
    hd = FNET_W // 4
    cc = np.arange(hd)
    ang = two_pi * ((cc[:, None] * cc[None, :]) % hd) / hd
    eye = np.eye(4)
    dftc = np.concatenate([np.kron(eye, np.cos(ang)), np.kron(eye, np.sin(ang))], axis=0) / np.sqrt(hd)
    ka = np.kron(fa, np.eye(ROW_TILE))
    o = np.arange(ROW_TILE * ROW_TILE)
    swap = np.eye(ROW_TILE * ROW_TILE)[(o % ROW_TILE) * ROW_TILE + o // ROW_TILE]
    return tuple(np.asarray(t, dtype=np.float32) for t in (ka, fb, dftc, swap))


@functools.lru_cache(maxsize=None)
def _pool_edge_scales(seq):
    hl = POOL_HALO
    pos = np.concatenate([np.arange(hl), np.arange(seq - hl, seq)])[:, None]
    win = np.repeat(np.asarray(POOL_WINDOWS), POOL_W // len(POOL_WINDOWS))[None, :]
    left = win // 2
    right = win - 1 - left
    cnt = np.minimum(pos + right, seq - 1) - np.maximum(pos - left, 0) + 1
    return (1.0 / cnt).astype(np.float32).reshape(2, hl, POOL_W)


def _block_diag(w):
    l, g, c, _ = w.shape
    eye = jnp.eye(g, dtype=w.dtype)
    return (eye[None, :, None, :, None] * w[:, :, :, None, :]).reshape(l, g * c, g * c)


def _in_proj_kernel(x_ref, g_ref, w_ref, sg_ref, za_ref, zu_ref, zv_ref, zc_ref, wb_ref, *, layer):
    @pl.when(pl.program_id(0) == 0)
    def _():
        wb_ref[...] = (w_ref[...] * g_ref[...]).astype(BF16)

    sg = _row(sg_ref, layer)
    rows = x_ref.shape[0] // IN_SUB
    for sub in range(IN_SUB):
        r = slice(sub * rows, (sub + 1) * rows)
        x = x_ref[r, :]
        z = _dot(x.astype(BF16), wb_ref[...]) * _rms_scale(x)
        za_ref[r, :] = z[:, :POOL_W].astype(BF16)
        zu_ref[r, :] = z[:, POOL_W:POOL_W + SGU_W].astype(BF16)
        v = z[:, POOL_W + SGU_W:POOL_W + 2 * SGU_W]
        vc = v - jnp.mean(v, axis=-1, keepdims=True)
        vn = vc * lax.rsqrt(jnp.mean(vc * vc, axis=-1, keepdims=True) + LN_EPS) * sg
        zv_ref[r, :] = vn.astype(BF16)
        zc_ref[r, :] = z[:, POOL_W + 2 * SGU_W:].astype(BF16)


def _in_proj(x2, g, w, sg, layer):
    n = x2.shape[0]
    t = IN_TILE
    row = lambda w_: pl.BlockSpec((t, w_), lambda i: (i, 0))
    return pl.pallas_call(
        functools.partial(_in_proj_kernel, layer=layer),
        grid=(n // t,),
        in_specs=[row(D), _layer_spec((D, 1), layer), _layer_spec((D, IN_W), layer), _const_spec(sg.shape)],
        out_specs=[row(POOL_W), row(SGU_W), row(SGU_W), row(FNET_W)],
        out_shape=[jax.ShapeDtypeStruct((n, POOL_W), BF16), jax.ShapeDtypeStruct((n, SGU_W), BF16),
                   jax.ShapeDtypeStruct((n, SGU_W), BF16), jax.ShapeDtypeStruct((n, FNET_W), BF16)],
        scratch_shapes=[pltpu.VMEM((D, IN_W), BF16)],
        compiler_params=_params(1),
        name="in_proj",
    )(x2, g, w, sg)


def _fft_a_kernel(ka_ref, x_ref, o_ref):
    rt = ROW_TILE
    c = x_ref.shape[-1]
    x = x_ref[0].reshape(FFT_N1 * rt, c)
    a = _dot(ka_ref[...], x).astype(BF16)
    o_ref[0] = a.reshape(2, FFT_K1_HALF, rt, c)


def _fft_stage_a(x4, ka):
    b, n1, n2, c = x4.shape
    rt = ROW_TILE
    kh = FFT_K1_HALF
    return pl.pallas_call(
        _fft_a_kernel,
        grid=(b, n2 // rt),
        in_specs=[_const_spec((2 * kh * rt, n1 * rt)),
                  pl.BlockSpec((1, n1, rt, c), lambda i, g: (i, 0, g, 0))],
        out_specs=pl.BlockSpec((1, 2, kh, rt, c), lambda i, g: (i, 0, 0, g, 0)),
        out_shape=jax.ShapeDtypeStruct((b, 2, kh, n2, c), BF16),
        compiler_params=_params(2),
        name="fft_stage_a",
    )(ka, x4)


def _fft_b_kernel(fb_ref, swap_ref, *rest):
    rt = ROW_TILE
    a_refs, (fr_ref, fi_ref, y_ref) = rest[:rt], rest[rt:]
    c = fr_ref.shape[-1]
    for j in range(rt):
        d = jnp.concatenate([a_refs[j][0, 0, 0], a_refs[j][0, 1, 0]], axis=0)
        y_ref[j] = _dot(fb_ref[j], d).astype(BF16)
    for t in range(FFT_N2 // rt):
        re = y_ref[:, t * rt:(t + 1) * rt, :].reshape(rt * rt, c)
        im = y_ref[:, FFT_N2 + t * rt:FFT_N2 + (t + 1) * rt, :].reshape(rt * rt, c)
        q = _dot(swap_ref[...], jnp.concatenate([re, im], axis=1)).astype(BF16)
        fr_ref[0, t * rt:(t + 1) * rt] = q[:, :c].reshape(rt, rt, c)
        fi_ref[0, t * rt:(t + 1) * rt] = q[:, c:].reshape(rt, rt, c)


def _fft_stage_b(a5, fb, swap):
    b, _, _, n2, c = a5.shape
    n1 = FFT_N1
    rt = ROW_TILE

    def slab(j):
        def index(g, i):
            k1 = g * rt + j
            return (i, 0, jnp.where(k1 > n1 // 2, n1 - k1, k1), 0, 0)
        return pl.BlockSpec((1, 2, 1, n2, c), index)

    out = pl.BlockSpec((1, n2, rt, c), lambda g, i: (i, 0, g, 0))
    return pl.pallas_call(
        _fft_b_kernel,
        grid=(n1 // rt, b),
        in_specs=[pl.BlockSpec((rt, 2 * n2, 2 * n2), lambda g, i: (g, 0, 0)),
                  _const_spec((rt * rt, rt * rt))] + [slab(j) for j in range(rt)],
        out_specs=[out, out],
        out_shape=[jax.ShapeDtypeStruct((b, n2, n1, c), BF16)] * 2,
        scratch_shapes=[pltpu.VMEM((rt, 2 * n2, c), BF16)],
        compiler_params=_params(2),
        name="fft_stage_b",
    )(fb, swap, *([a5] * rt))


def _fold_kernel(dftc_ref, fw_ref, wo_ref, o_ref):
    kd = POOL_W + SGU_W
    o_ref[0:kd, :] = wo_ref[0:kd, :].astype(BF16)
    t = _dot(dftc_ref[...], fw_ref[...]).astype(BF16)
    o_ref[kd:, :] = _dot(t, wo_ref[kd:, :].astype(BF16)).astype(BF16)


def _fold(dftc, fw_bd, w_out):
    depth = fw_bd.shape[0]
    return pl.pallas_call(
        _fold_kernel,
        grid=(depth,),
        in_specs=[_const_spec((2 * FNET_W, FNET_W)),
                  pl.BlockSpec((None, FNET_W, FNET_W), lambda l: (l, 0, 0)),
                  pl.BlockSpec((None, D, D), lambda l: (l, 0, 0))],
        out_specs=pl.BlockSpec((None, MIX_K, D), lambda l: (l, 0, 0)),
        out_shape=jax.ShapeDtypeStruct((depth, MIX_K, D), BF16),
        compiler_params=_params(1),
        name="fold_fourier",
    )(dftc, fw_bd, w_out)


def _mix_out_kernel(x_ref, za_ref, zp_ref, zn_ref, zu_ref, zv_ref, fr_ref, fi_ref,
                    pw_ref, ps_ref, sw_ref, sb_ref, wc_ref, g_ref, edge_ref, o_ref, ext_ref, ym_ref,
                    *, seq, layer):
    pool_scale, g_post = _row(ps_ref, layer), _row(g_ref, layer)
    t = MIX_TILE
    hl = POOL_HALO
    rows = t // MIX_SUB
    ext_rows = rows + 2 * hl
    j = pl.program_id(1)
    nj = seq // t
    ext_ref[0:hl] = jnp.where(j > 0, zp_ref[...].astype(F32), 0.0)
    ext_ref[hl:hl + t] = za_ref[...].astype(F32)
    ext_ref[hl + t:2 * hl + t] = jnp.where(j < nj - 1, zn_ref[...].astype(F32), 0.0)

    lane = lax.broadcasted_iota(jnp.int32, (1, 128), 1)
    low = lane < 64
    inv_win = jnp.concatenate([jnp.where(low, 1.0 / 2, 1.0 / 4), jnp.where(low, 1.0 / 8, 1.0 / 16)], axis=1)
    first_scale = jnp.where(j == 0, edge_ref[0], inv_win)
    last_scale = jnp.where(j == nj - 1, edge_ref[1], inv_win)
    nch = rows // SGU_CHUNK
    hdim = SGU_W // SGU_HEADS

    def down(a, k):
        return pltpu.roll(a, k, 0)

    def up(a, k):
        return pltpu.roll(a, ext_rows - k, 0)

    for sub in range(MIX_SUB):
        r0 = sub * rows
        r = slice(r0, r0 + rows)
        e = ext_ref[r0:r0 + ext_rows, :]
        c2 = e + down(e, 1)
        c4 = c2 + down(c2, 2)
        c4b = c4[:, 128:]
        c8 = c4b + down(c4b, 4)
        c16 = c8 + down(c8, 8)
        s0 = jnp.where(low, c2[:, :128], up(c4[:, :128], 1))
        s1 = jnp.where(low, up(c8, 3), up(c16, 7))
        ws = jnp.concatenate([s0, s1], axis=1)[hl:hl + rows]
        scale_head = first_scale if sub == 0 else inv_win
        scale_tail = last_scale if sub == MIX_SUB - 1 else inv_win
        pooled = jnp.concatenate([ws[:hl] * scale_head, ws[hl:rows - hl] * inv_win,
                                  ws[rows - hl:] * scale_tail], axis=0)
        diff = (pooled - e[hl:hl + rows]).astype(BF16)
        ya = _dot(diff, pw_ref[...]) * pool_scale
        ym_ref[r, 0:POOL_W] = ya.astype(BF16)

        for h in range(SGU_HEADS):
            cols = slice(h * hdim, (h + 1) * hdim)
            vh = jnp.concatenate(
                [zv_ref[r0 + c * SGU_CHUNK:r0 + (c + 1) * SGU_CHUNK, cols] for c in range(nch)], axis=1)
            mh = _dot(sw_ref[h], vh)
            for c in range(nch):
                rc = slice(r0 + c * SGU_CHUNK, r0 + (c + 1) * SGU_CHUNK)
                mixed = mh[:, c * hdim:(c + 1) * hdim] + sb_ref[h]
                ym_ref[rc, POOL_W + h * hdim:POOL_W + (h + 1) * hdim] = (
                    zu_ref[rc, cols].astype(F32) * mixed).astype(BF16)

        ym_ref[r, POOL_W + SGU_W:POOL_W + SGU_W + FNET_W] = fr_ref[r, :]
        ym_ref[r, POOL_W + SGU_W + FNET_W:] = fi_ref[r, :]
        y = _dot(ym_ref[r, :], wc_ref[...])
        o_ref[r, :] = x_ref[r, :] + _rms(y, g_post)


def _mix_out(x2, za, zu, zv, fr, fi, pw_bd, ps, sw, sb_full, wcat, g, edge, layer, *, batch, seq):
    n = x2.shape[0]
    t = MIX_TILE
    hl = POOL_HALO
    nj = seq // t
    row = lambda w_: pl.BlockSpec((t, w_), lambda b, j: (b * nj + j, 0))
    prev = pl.BlockSpec((hl, POOL_W), lambda b, j: (jnp.maximum((b * nj + j) * (t // hl) - 1, 0), 0))
    nxt = pl.BlockSpec((hl, POOL_W), lambda b, j: (jnp.minimum((b * nj + j + 1) * (t // hl), n // hl - 1), 0))
    return pl.pallas_call(
        functools.partial(_mix_out_kernel, seq=seq, layer=layer),
        grid=(batch, nj),
        in_specs=[row(D), row(POOL_W), prev, nxt, row(SGU_W), row(SGU_W), row(FNET_W), row(FNET_W),
                  _layer_spec((POOL_W, POOL_W), layer), _const_spec(ps.shape),
                  _layer_spec((SGU_HEADS, SGU_CHUNK, SGU_CHUNK), layer),
                  _layer_spec((SGU_HEADS, SGU_CHUNK, SGU_CHUNK), layer),
                  _layer_spec((MIX_K, D), layer), _const_spec(g.shape),
                  _const_spec((2, hl, POOL_W))],
        out_specs=row(D),
        out_shape=jax.ShapeDtypeStruct((n, D), F32),
        scratch_shapes=[pltpu.VMEM((t + 2 * hl, POOL_W), F32), pltpu.VMEM((t, MIX_K), BF16)],
        compiler_params=_params(2),
        name="mix_out",
    )(x2, za, za, za, zu, zv, fr, fi, pw_bd, ps, sw, sb_full, wcat, g, edge)


def _kv_kernel(m_ref, g_ref, wk_ref, wv_ref, k_ref, v_ref, wkb_ref, wvb_ref):
    @pl.when(pl.program_id(1) == 0)
    def _():
        wkb_ref[...] = wk_ref[...].astype(BF16)
        wvb_ref[...] = wv_ref[...].astype(BF16)

    g = g_ref[pl.ds(pl.program_id(0), 1), :]
    m = _rms(m_ref[0], g).astype(BF16)
    k_ref[...] = _dot(m, wkb_ref[...]).astype(BF16)
    v_ref[...] = _dot(m, wvb_ref[...]).astype(BF16)


def _kv_proj(mem, g, wk, wv):
    b = mem.shape[0]
    depth = wk.shape[0]
    weight = pl.BlockSpec((None, D, D), lambda l, i: (l, 0, 0))
    out = pl.BlockSpec((None, None, MEM_LEN, D), lambda l, i: (l, i, 0, 0))
    return pl.pallas_call(
        _kv_kernel,
        grid=(depth, b),
        in_specs=[pl.BlockSpec((1, MEM_LEN, D), lambda l, i: (i, 0, 0)), _const_spec(g.shape), weight, weight],
        out_specs=[out, out],
        out_shape=[jax.ShapeDtypeStruct((depth, b, MEM_LEN, D), BF16)] * 2,
        scratch_shapes=[pltpu.VMEM((D, D), BF16)] * 2,
        compiler_params=_params(2),
        name="kv_proj",
    )(mem, g, wk, wv)


def _attn_rows(x, k_ref, v_ref, wq_ref, wo_ref, g_post, oh_ref, r):
    q = (_dot(x.astype(BF16), wq_ref[...]) * (_rms_scale(x) * XA_HD ** -0.5)).astype(BF16)
    for hd in range(XA_HEADS):
        sl = slice(hd * XA_HD, (hd + 1) * XA_HD)
        s = lax.dot_general(q[:, sl], k_ref[:, sl], (((1,), (1,)), ((), ())),
                            preferred_element_type=F32)
        e = jnp.exp(s - jnp.max(s, axis=-1, keepdims=True))
        o = _dot(e.astype(BF16), v_ref[:, sl]) / jnp.sum(e, axis=-1, keepdims=True)
        oh_ref[r, sl] = o.astype(BF16)
    y = _dot(oh_ref[r, :], wo_ref[...])
    return x + _rms(y, g_post)


def _ffn_rows(x, wg_ref, wu_ref, wd_ref, g_post):
    h = x.astype(BF16)
    scale = _rms_scale(x)
    y = None
    off = 0
    for ch in FFN_CHUNKS:
        g = _dot(h, wg_ref[:, off:off + ch]) * scale
        u = _dot(h, wu_ref[:, off:off + ch]) * scale
        a = (g * jax.nn.sigmoid(g) * u).astype(BF16)
        part = _dot(a, wd_ref[off:off + ch, :])
        y = part if y is None else y + part
        off += ch
    return x + _rms(y, g_post)


def _attn_kernel(x_ref, k_ref, v_ref, gpre_ref, wq_ref, wo_ref, gpost_ref, o_ref, oh_ref, wqb_ref, wob_ref,
                 *, layer):
    @pl.when((pl.program_id(0) == 0) & (pl.program_id(1) == 0))
    def _():
        wqb_ref[...] = (wq_ref[...] * gpre_ref[...]).astype(BF16)
        wob_ref[...] = wo_ref[...].astype(BF16)

    g_post = _row(gpost_ref, layer)
    rows = x_ref.shape[0] // ATT_SUB
    for sub in range(ATT_SUB):
        r = slice(sub * rows, (sub + 1) * rows)
        o_ref[r, :] = _attn_rows(x_ref[r, :], k_ref, v_ref, wqb_ref, wob_ref, g_post, oh_ref, r)


def _attention(x2, k, v, gpre, wq, wo, gpost, layer, *, batch, seq):
    n = x2.shape[0]
    t = ATT_TILE
    nj = seq // t
    row = pl.BlockSpec((t, D), lambda b, j: (b * nj + j, 0))
    kv = pl.BlockSpec((None, None, MEM_LEN, D), lambda b, j: (layer, b, 0, 0))
    return pl.pallas_call(
        functools.partial(_attn_kernel, layer=layer),
        grid=(batch, nj),
        in_specs=[row, kv, kv, _layer_spec((D, 1), layer), _layer_spec((D, D), layer),
                  _layer_spec((D, D), layer), _const_spec(gpost.shape)],
        out_specs=row,
        out_shape=jax.ShapeDtypeStruct((n, D), F32),
        scratch_shapes=[pltpu.VMEM((t, D), BF16), pltpu.VMEM((D, D), BF16), pltpu.VMEM((D, D), BF16)],
        compiler_params=_params(2),
        name="mem_attention",
    )(x2, k, v, gpre, wq, wo, gpost)


def _ffn_kernel(x_ref, wg_ref, wu_ref, wd_ref, gpost_ref, o_ref, *, layer):
    g_post = _row(gpost_ref, layer)
    rows = x_ref.shape[0] // FFN_SUB
    for sub in range(FFN_SUB):
        r = slice(sub * rows, (sub + 1) * rows)
        o_ref[r, :] = _ffn_rows(x_ref[r, :], wg_ref, wu_ref, wd_ref, g_post)


def _ffn(x2, wg, wu, wd, gpost, layer):
    n = x2.shape[0]
    t = FFN_TILE
    row = pl.BlockSpec((t, D), lambda i: (i, 0))
    return pl.pallas_call(
        functools.partial(_ffn_kernel, layer=layer),
        grid=(n // t,),
        in_specs=[row, _layer_spec((D, FFN_H), layer), _layer_spec((D, FFN_H), layer),
                  _layer_spec((FFN_H, D), layer), _const_spec(gpost.shape)],
        out_specs=row,
        out_shape=jax.ShapeDtypeStruct((n, D), F32),
        compiler_params=_params(1),
        name="swiglu",
    )(x2, wg, wu, wd, gpost)


def _cast_kernel(w_ref, g_ref, o_ref):
    o_ref[...] = (w_ref[...] * g_ref[...]).astype(BF16)


def _scaled_cast(w, gcol):
    depth, r, c = w.shape
    rb = CAST_ROWS
    assert r % rb == 0
    return pl.pallas_call(
        _cast_kernel,
        grid=(depth, r // rb),
        in_specs=[pl.BlockSpec((None, rb, c), lambda l, i: (l, i, 0)),
                  pl.BlockSpec((None, rb, 1), lambda l, i: (l, i, 0))],
        out_specs=pl.BlockSpec((None, rb, c), lambda l, i: (l, i, 0)),
        out_shape=jax.ShapeDtypeStruct(w.shape, BF16),
        compiler_params=_params(2),
        name="scaled_cast",
    )(w, gcol)


def kernel(x, mem, ln_mix_pre, w_in, pool_w, pool_scale, sgu_norm, sgu_w, sgu_b, fnet_w, w_out,
           ln_mix_post, ln_xa_pre, ln_mem, xa_wq, xa_wk, xa_wv, xa_wo, ln_xa_post, ln_ffn_pre,
           ffn_wg, ffn_wu, ffn_wd, ln_ffn_post):
    batch, seq, d = x.shape
    depth = w_in.shape[0]
    assert d == D and sum(FFN_CHUNKS) == FFN_H
    assert all(seq % t == 0 for t in (IN_TILE, MIX_TILE, ATT_TILE, FFN_TILE))
    assert (MIX_TILE // MIX_SUB) % SGU_CHUNK == 0
    ka, fb, dftc, swap = (jnp.asarray(t).astype(BF16) for t in _fft_tables(seq))
    edge = jnp.asarray(_pool_edge_scales(seq))
    n = batch * seq
    x2 = x.reshape(n, d)
    bf = lambda a: a.astype(BF16)
    col = lambda g: g[:, :, None]

    wg_b = _scaled_cast(ffn_wg, col(ln_ffn_pre))
    wu_b = _scaled_cast(ffn_wu, col(ln_ffn_pre))
    wd_b = bf(ffn_wd)
    pw_bd = bf(_block_diag(pool_w))
    sw_b = bf(sgu_w)
    sb_full = jnp.broadcast_to(sgu_b[:, :, :, None], sgu_b.shape + (SGU_CHUNK,))
    wcat = _fold(dftc, bf(_block_diag(fnet_w)), w_out)
    k_all, v_all = _kv_proj(mem, ln_mem, xa_wk, xa_wv)

    for l in range(depth):
        za, zu, zv, zc = _in_proj(x2, col(ln_mix_pre), w_in, sgu_norm, l)
        a5 = _fft_stage_a(zc.reshape(batch, FFT_N1, FFT_N2, FNET_W), ka)
        fr, fi = _fft_stage_b(a5, fb, swap)
        x2 = _mix_out(x2, za, zu, zv, fr.reshape(n, FNET_W), fi.reshape(n, FNET_W), pw_bd, pool_scale, sw_b,
                      sb_full, wcat, ln_mix_post, edge, l, batch=batch, seq=seq)
        x2 = _attention(x2, k_all, v_all, col(ln_xa_pre), xa_wq, xa_wo, ln_xa_post, l, batch=batch, seq=seq)
        x2 = _ffn(x2, wg_b, wu_b, wd_b, ln_ffn_post, l)
    return x2.reshape(batch, seq, d)
```

```python
import functools

import numpy as np
import jax
import jax.numpy as jnp
from jax import lax
from jax.experimental import pallas as pl
from jax.experimental.pallas import tpu as pltpu

F32 = jnp.float32
BF16 = jnp.bfloat16

D = 1024
POOL_W = 256
POOL_WINDOWS = (2, 4, 8, 16)
SGU_W = 512
FNET_W = 256
IN_W = POOL_W + 2 * SGU_W + FNET_W
MIX_K = POOL_W + SGU_W + 2 * FNET_W
SGU_CHUNK = 128
SGU_HEADS = 4
MEM_LEN = 256
XA_HEADS = 4
XA_HD = D // XA_HEADS
FFN_H = 2816
RMS_EPS = 1e-6
LN_EPS = 1e-5

FFT_N1 = 64
FFT_N2 = 128
FFT_K1_HALF = FFT_N1 // 2 + 1
ROW_TILE = 16
FFT_A_TILES = 4
POOL_HALO = 16
IN_TILE, IN_SUB = 2048, 4
MIX_TILE, MIX_SUB = 1024, 2
FFN_TILE, FFN_SUB = 1024, 2
FFN_CHUNKS = (1536, 1280)
VMEM_LIMIT = 56 * 1024 * 1024


def _rms_scale(xf):
    return lax.rsqrt(jnp.mean(xf * xf, axis=-1, keepdims=True) + RMS_EPS)


def _rms(xf, g):
    return xf * _rms_scale(xf) * g


def _dot(a, b):
    return jnp.dot(a, b, preferred_element_type=F32)


def _params(n_axes):
    return pltpu.CompilerParams(
        dimension_semantics=("arbitrary",) * n_axes, vmem_limit_bytes=VMEM_LIMIT)


def _const_spec(shape):
    nd = len(shape)
    return pl.BlockSpec(shape, lambda *_: (0,) * nd, pipeline_mode=pl.Buffered(1))


def _layer_spec(shape, layer):
    nd = len(shape)
    return pl.BlockSpec((None,) + tuple(shape), lambda *_: (layer,) + (0,) * nd,
                        pipeline_mode=pl.Buffered(1))


def _row(ref, layer):
    return ref[layer:layer + 1, :]


@functools.lru_cache(maxsize=None)
def _fft_tables(seq):
    assert seq == FFT_N1 * FFT_N2
    two_pi = 2.0 * np.pi
    k1 = np.arange(FFT_N1)
    ang = two_pi * ((k1[:, None] * k1[None, :]) % FFT_N1) / FFT_N1
    ang = ang[:FFT_N1 // 2 + 1]
    fa = np.concatenate([np.cos(ang), -np.sin(ang)], axis=0) / np.sqrt(FFT_N1)
    k2 = np.arange(FFT_N2)
    k = k1[:, None, None] + FFT_N1 * k2[None, :, None]
    ph = two_pi * ((k * k2[None, None, :]) % seq) / seq
    c, s = np.cos(ph), np.sin(ph)
    sign = np.where(k1 > FFT_N1 // 2, -1.0, 1.0)[:, None, None]
    fb = np.concatenate([np.concatenate([c, s * sign], axis=2),
                         np.concatenate([-s, c * sign], axis=2)], axis=1) / np.sqrt(FFT_N2)
    hd = FNET_W // 4
    cc = np.arange(hd)
    ang = two_pi * ((cc[:, None] * cc[None, :]) % hd) / hd
    eye = np.eye(4)
    dftc = np.concatenate([np.kron(eye, np.cos(ang)), np.kron(eye, np.sin(ang))], axis=0) / np.sqrt(hd)
    ka = np.kron(fa, np.eye(ROW_TILE))
    o = np.arange(ROW_TILE * ROW_TILE)
    swap = np.eye(ROW_TILE * ROW_TILE)[(o % ROW_TILE) * ROW_TILE + o // ROW_TILE]
    return tuple(np.asarray(t, dtype=np.float32) for t in (ka, fb, dftc, swap))


@functools.lru_cache(maxsize=None)
def _pool_edge_scales(seq):
    hl = POOL_HALO
    pos = np.concatenate([np.arange(hl), np.arange(seq - hl, seq)])[:, None]
    win = np.repeat(np.asarray(POOL_WINDOWS), POOL_W // len(POOL_WINDOWS))[None, :]
    left = win // 2
    right = win - 1 - left
    cnt = np.minimum(pos + right, seq - 1) - np.maximum(pos - left, 0) + 1
    return (1.0 / cnt).astype(np.float32).reshape(2, hl, POOL_W)


def _block_diag(w):
    l, g, c, _ = w.shape
    eye = jnp.eye(g, dtype=w.dtype)
    return (eye[None, :, None, :, None] * w[:, :, :, None, :]).reshape(l, g * c, g * c)


def _in_proj_kernel(x_ref, g_ref, w_ref, sg_ref, wg_ref, wu_ref, wd_ref, gf_ref,
                    wq_ref, wo_ref, ga_ref,
                    za_ref, zu_ref, zv_ref, zc_ref, wgb_ref, wub_ref, wdb_ref, wqb_ref, wob_ref, wb_ref,
                    *, layer):
    @pl.when(pl.program_id(0) == 0)
    def _():
        wb_ref[...] = (w_ref[...] * g_ref[...]).astype(BF16)

    wgb_ref[...] = (wg_ref[...] * gf_ref[...]).astype(BF16)
    wub_ref[...] = (wu_ref[...] * gf_ref[...]).astype(BF16)
    wdb_ref[...] = wd_ref[...].astype(BF16)
    wqb_ref[...] = (wq_ref[...] * ga_ref[...]).astype(BF16)
    wob_ref[...] = wo_ref[...].astype(BF16)

    sg = _row(sg_ref, layer)
    rows = x_ref.shape[0] // IN_SUB
    for sub in range(IN_SUB):
        r = slice(sub * rows, (sub + 1) * rows)
        x = x_ref[r, :]
        z = _dot(x.astype(BF16), wb_ref[...]) * _rms_scale(x)
        za_ref[r, :] = z[:, :POOL_W].astype(BF16)
        zu_ref[r, :] = z[:, POOL_W:POOL_W + SGU_W].astype(BF16)
        v = z[:, POOL_W + SGU_W:POOL_W + 2 * SGU_W]
        vc = v - jnp.mean(v, axis=-1, keepdims=True)
        vn = vc * lax.rsqrt(jnp.mean(vc * vc, axis=-1, keepdims=True) + LN_EPS) * sg
        zv_ref[r, :] = vn.astype(BF16)
        zc_ref[r, :] = z[:, POOL_W + 2 * SGU_W:].astype(BF16)


def _in_proj(x2, g, w, sg, wg, wu, wd, gf, wq, wo, ga, layer):
    n = x2.shape[0]
    t = IN_TILE
    steps = n // t
    dr, hr = D // steps, FFN_H // steps
    assert dr * steps == D and hr * steps == FFN_H and dr % ROW_TILE == 0 and hr % ROW_TILE == 0
    row = lambda w_: pl.BlockSpec((t, w_), lambda i: (i, 0))
    wrows = lambda r, c: pl.BlockSpec((None, r, c), lambda i: (layer, i, 0))
    orows = lambda r, c: pl.BlockSpec((r, c), lambda i: (i, 0))
    return pl.pallas_call(
        functools.partial(_in_proj_kernel, layer=layer),
        grid=(steps,),
        in_specs=[row(D), _layer_spec((D, 1), layer), _layer_spec((D, IN_W), layer), _const_spec(sg.shape),
                  wrows(dr, FFN_H), wrows(dr, FFN_H), wrows(hr, D), wrows(dr, 1),
                  wrows(dr, D), wrows(dr, D), wrows(dr, 1)],
        out_specs=[row(POOL_W), row(SGU_W), row(SGU_W), row(FNET_W),
                   orows(dr, FFN_H), orows(dr, FFN_H), orows(hr, D), orows(dr, D), orows(dr, D)],
        out_shape=[jax.ShapeDtypeStruct((n, POOL_W), BF16), jax.ShapeDtypeStruct((n, SGU_W), BF16),
                   jax.ShapeDtypeStruct((n, SGU_W), BF16), jax.ShapeDtypeStruct((n, FNET_W), BF16),
                   jax.ShapeDtypeStruct((D, FFN_H), BF16), jax.ShapeDtypeStruct((D, FFN_H), BF16),
                   jax.ShapeDtypeStruct((FFN_H, D), BF16), jax.ShapeDtypeStruct((D, D), BF16),
                   jax.ShapeDtypeStruct((D, D), BF16)],
        scratch_shapes=[pltpu.VMEM((D, IN_W), BF16)],
        compiler_params=_params(1),
        name="in_proj",
    )(x2, g, w, sg, wg, wu, wd, gf, wq, wo, ga)


def _fft_a_kernel(ka_ref, x_ref, o_ref):
    rt = ROW_TILE
    c = x_ref.shape[-1]
    for t in range(FFT_A_TILES):
        x = x_ref[0, :, t * rt:(t + 1) * rt, :].reshape(FFT_N1 * rt, c)
        a = _dot(ka_ref[...], x).astype(BF16)
        o_ref[0, :, :, t * rt:(t + 1) * rt, :] = a.reshape(2, FFT_K1_HALF, rt, c)


def _fft_stage_a(x4, ka):
    b, n1, n2, c = x4.shape
    rt = ROW_TILE
    kh = FFT_K1_HALF
    rows = rt * FFT_A_TILES
    return pl.pallas_call(
        _fft_a_kernel,
        grid=(b, n2 // rows),
        in_specs=[_const_spec((2 * kh * rt, n1 * rt)),
                  pl.BlockSpec((1, n1, rows, c), lambda i, g: (i, 0, g, 0))],
        out_specs=pl.BlockSpec((1, 2, kh, rows, c), lambda i, g: (i, 0, 0, g, 0)),
        out_shape=jax.ShapeDtypeStruct((b, 2, kh, n2, c), BF16),
        compiler_params=_params(2),
        name="fft_stage_a",
    )(ka, x4)


def _fft_b_kernel(fb_ref, swap_ref, *rest):
    rt = ROW_TILE
    a_refs, (fr_ref, fi_ref, y_ref) = rest[:rt], rest[rt:]
    c = fr_ref.shape[-1]
    for j in range(rt):
        d = jnp.concatenate([a_refs[j][0, 0, 0], a_refs[j][0, 1, 0]], axis=0)
        y_ref[j] = _dot(fb_ref[j], d).astype(BF16)
    for t in range(FFT_N2 // rt):
        re = y_ref[:, t * rt:(t + 1) * rt, :].reshape(rt * rt, c)
        im = y_ref[:, FFT_N2 + t * rt:FFT_N2 + (t + 1) * rt, :].reshape(rt * rt, c)
        q = _dot(swap_ref[...], jnp.concatenate([re, im], axis=1)).astype(BF16)
        fr_ref[0, t * rt:(t + 1) * rt] = q[:, :c].reshape(rt, rt, c)
        fi_ref[0, t * rt:(t + 1) * rt] = q[:, c:].reshape(rt, rt, c)


def _fft_stage_b(a5, fb, swap):
    b, _, _, n2, c = a5.shape
    n1 = FFT_N1
    rt = ROW_TILE

    def slab(j):
        def index(g, i):
            k1 = g * rt + j
            return (i, 0, jnp.where(k1 > n1 // 2, n1 - k1, k1), 0, 0)
        return pl.BlockSpec((1, 2, 1, n2, c), index)

    out = pl.BlockSpec((1, n2, rt, c), lambda g, i: (i, 0, g, 0))
    return pl.pallas_call(
        _fft_b_kernel,
        grid=(n1 // rt, b),
        in_specs=[pl.BlockSpec((rt, 2 * n2, 2 * n2), lambda g, i: (g, 0, 0)),
                  _const_spec((rt * rt, rt * rt))] + [slab(j) for j in range(rt)],
        out_specs=[out, out],
        out_shape=[jax.ShapeDtypeStruct((b, n2, n1, c), BF16)] * 2,
        scratch_shapes=[pltpu.VMEM((rt, 2 * n2, c), BF16)],
        compiler_params=_params(2),
        name="fft_stage_b",
    )(fb, swap, *([a5] * rt))


def _fold_kernel(dftc_ref, fw_ref, pw_ref, ps_ref, wo_ref, o_ref):
    kd = POOL_W + SGU_W
    ps = ps_ref[pl.ds(pl.program_id(0), 1), :]
    o_ref[0:POOL_W, :] = _dot((pw_ref[...] * ps).astype(BF16), wo_ref[0:POOL_W, :].astype(BF16)).astype(BF16)
    o_ref[POOL_W:kd, :] = wo_ref[POOL_W:kd, :].astype(BF16)
    t = _dot(dftc_ref[...], fw_ref[...]).astype(BF16)
    o_ref[kd:, :] = _dot(t, wo_ref[kd:, :].astype(BF16)).astype(BF16)


def _fold(dftc, fw_bd, pw_bd, ps, w_out):
    depth = fw_bd.shape[0]
    return pl.pallas_call(
        _fold_kernel,
        grid=(depth,),
        in_specs=[_const_spec((2 * FNET_W, FNET_W)),
                  pl.BlockSpec((None, FNET_W, FNET_W), lambda l: (l, 0, 0)),
                  pl.BlockSpec((None, POOL_W, POOL_W), lambda l: (l, 0, 0)),
                  _const_spec(ps.shape),
                  pl.BlockSpec((None, D, D), lambda l: (l, 0, 0))],
        out_specs=pl.BlockSpec((None, MIX_K, D), lambda l: (l, 0, 0)),
        out_shape=jax.ShapeDtypeStruct((depth, MIX_K, D), BF16),
        compiler_params=_params(1),
        name="fold_weights",
    )(dftc, fw_bd, pw_bd, ps, w_out)


def _mix_attn_kernel(x_ref, za_ref, zp_ref, zn_ref, zu_ref, zv_ref, fr_ref, fi_ref,
                     sw_ref, sb_ref, wc_ref, g_ref, edge_ref, kt_ref, v_ref, wq_ref, wo_ref, ga_ref,
                     o_ref, ext_ref, ym_ref, oh_ref, *, seq, layer):
    g_post, ga_post = _row(g_ref, layer), _row(ga_ref, layer)
    t = MIX_TILE
    hl = POOL_HALO
    rows = t // MIX_SUB
    ext_rows = rows + 2 * hl
    j = pl.program_id(1)
    nj = seq // t
    ext_ref[0:hl] = jnp.where(j > 0, zp_ref[...].astype(F32), 0.0)
    ext_ref[hl:hl + t] = za_ref[...].astype(F32)
    ext_ref[hl + t:2 * hl + t] = jnp.where(j < nj - 1, zn_ref[...].astype(F32), 0.0)

    lane = lax.broadcasted_iota(jnp.int32, (1, 128), 1)
    low = lane < 64
    inv_win = jnp.concatenate([jnp.where(low, 1.0 / 2, 1.0 / 4), jnp.where(low, 1.0 / 8, 1.0 / 16)], axis=1)
    first_scale = jnp.where(j == 0, edge_ref[0], inv_win)
    last_scale = jnp.where(j == nj - 1, edge_ref[1], inv_win)
    nch = rows // SGU_CHUNK
    hdim = SGU_W // SGU_HEADS

    def down(a, k):
        return pltpu.roll(a, k, 0)

    def up(a, k):
        return pltpu.roll(a, ext_rows - k, 0)

    for sub in range(MIX_SUB):
        r0 = sub * rows
        r = slice(r0, r0 + rows)
        e = ext_ref[r0:r0 + ext_rows, :]
        c2 = e + down(e, 1)
        c4 = c2 + down(c2, 2)
        c4b = c4[:, 128:]
        c8 = c4b + down(c4b, 4)
        c16 = c8 + down(c8, 8)
        s0 = jnp.where(low, c2[:, :128], up(c4[:, :128], 1))
        s1 = jnp.where(low, up(c8, 3), up(c16, 7))
        ws = jnp.concatenate([s0, s1], axis=1)[hl:hl + rows]
        scale_head = first_scale if sub == 0 else inv_win
        scale_tail = last_scale if sub == MIX_SUB - 1 else inv_win
        pooled = jnp.concatenate([ws[:hl] * scale_head, ws[hl:rows - hl] * inv_win,
                                  ws[rows - hl:] * scale_tail], axis=0)
        ym_ref[r, 0:POOL_W] = (pooled - e[hl:hl + rows]).astype(BF16)

        for h in range(SGU_HEADS):
            cols = slice(h * hdim, (h + 1) * hdim)
            vh = jnp.concatenate(
                [zv_ref[r0 + c * SGU_CHUNK:r0 + (c + 1) * SGU_CHUNK, cols] for c in range(nch)], axis=1)
            mh = _dot(sw_ref[h], vh)
            for c in range(nch):
                rc = slice(r0 + c * SGU_CHUNK, r0 + (c + 1) * SGU_CHUNK)
                mixed = mh[:, c * hdim:(c + 1) * hdim] + sb_ref[h]
                ym_ref[rc, POOL_W + h * hdim:POOL_W + (h + 1) * hdim] = (
                    zu_ref[rc, cols].astype(F32) * mixed).astype(BF16)

        ym_ref[r, POOL_W + SGU_W:POOL_W + SGU_W + FNET_W] = fr_ref[r, :]
        ym_ref[r, POOL_W + SGU_W + FNET_W:] = fi_ref[r, :]
        y = _dot(ym_ref[r, :], wc_ref[...])
        x1 = x_ref[r, :] + _rms(y, g_post)
        o_ref[r, :] = _attn_rows(x1, kt_ref, v_ref, wq_ref, wo_ref, ga_post, oh_ref, r)


def _mix_attn(x2, za, zu, zv, fr, fi, sw, sb_full, wcat, g, edge, kt, v, wq, wo, ga, layer, *, batch, seq):
    n = x2.shape[0]
    t = MIX_TILE
    hl = POOL_HALO
    nj = seq // t
    row = lambda w_: pl.BlockSpec((t, w_), lambda b, j: (b * nj + j, 0))
    prev = pl.BlockSpec((hl, POOL_W), lambda b, j: (jnp.maximum((b * nj + j) * (t // hl) - 1, 0), 0))
    nxt = pl.BlockSpec((hl, POOL_W), lambda b, j: (jnp.minimum((b * nj + j + 1) * (t // hl), n // hl - 1), 0))
    keys = pl.BlockSpec((None, None, D, MEM_LEN), lambda b, j: (layer, b, 0, 0))
    vals = pl.BlockSpec((None, None, MEM_LEN, D), lambda b, j: (layer, b, 0, 0))
    return pl.pallas_call(
        functools.partial(_mix_attn_kernel, seq=seq, layer=layer),
        grid=(batch, nj),
        in_specs=[row(D), row(POOL_W), prev, nxt, row(SGU_W), row(SGU_W), row(FNET_W), row(FNET_W),
                  _layer_spec((SGU_HEADS, SGU_CHUNK, SGU_CHUNK), layer),
                  _layer_spec((SGU_HEADS, SGU_CHUNK, SGU_CHUNK), layer),
                  _layer_spec((MIX_K, D), layer), _const_spec(g.shape),
                  _const_spec((2, hl, POOL_W)),
                  keys, vals, _const_spec((D, D)), _const_spec((D, D)), _const_spec(ga.shape)],
        out_specs=row(D),
        out_shape=jax.ShapeDtypeStruct((n, D), F32),
        scratch_shapes=[pltpu.VMEM((t + 2 * hl, POOL_W), F32), pltpu.VMEM((t, MIX_K), BF16),
                        pltpu.VMEM((t, D), BF16)],
        compiler_params=_params(2),
        name="mix_attention",
    )(x2, za, za, za, zu, zv, fr, fi, sw, sb_full, wcat, g, edge, kt, v, wq, wo, ga)


def _kv_kernel(m_ref, g_ref, wk_ref, wv_ref, kt_ref, v_ref):
    b, mlen, d = m_ref.shape
    g = g_ref[pl.ds(pl.program_id(0), 1), :]
    m = _rms(m_ref[...].reshape(b * mlen, d), g).astype(BF16)
    k = _dot(m, wk_ref[...].astype(BF16))
    for i in range(b):
        kt_ref[i] = k[i * mlen:(i + 1) * mlen, :].T.astype(BF16)
    v_ref[...] = _dot(m, wv_ref[...].astype(BF16)).astype(BF16).reshape(b, mlen, d)


def _kv_proj(mem, g, wk, wv):
    b = mem.shape[0]
    depth = wk.shape[0]
    weight = pl.BlockSpec((None, D, D), lambda l: (l, 0, 0))
    return pl.pallas_call(
        _kv_kernel,
        grid=(depth,),
        in_specs=[_const_spec(mem.shape), _const_spec(g.shape), weight, weight],
        out_specs=[pl.BlockSpec((None, b, D, MEM_LEN), lambda l: (l, 0, 0, 0)),
                   pl.BlockSpec((None, b, MEM_LEN, D), lambda l: (l, 0, 0, 0))],
        out_shape=[jax.ShapeDtypeStruct((depth, b, D, MEM_LEN), BF16),
                   jax.ShapeDtypeStruct((depth, b, MEM_LEN, D), BF16)],
        compiler_params=_params(1),
        name="kv_proj",
    )(mem, g, wk, wv)


def _attn_rows(x, kt_ref, v_ref, wq_ref, wo_ref, g_post, oh_ref, r):
    q = (_dot(x.astype(BF16), wq_ref[...]) * (_rms_scale(x) * XA_HD ** -0.5)).astype(BF16)
    for hd in range(XA_HEADS):
        sl = slice(hd * XA_HD, (hd + 1) * XA_HD)
        s = _dot(q[:, sl], kt_ref[sl, :])
        e = jnp.exp(s - jnp.max(s, axis=-1, keepdims=True))
        o = _dot(e.astype(BF16), v_ref[:, sl]) / jnp.sum(e, axis=-1, keepdims=True)
        oh_ref[r, sl] = o.astype(BF16)
    y = _dot(oh_ref[r, :], wo_ref[...])
    return x + _rms(y, g_post)


def _ffn_rows(x, wg_ref, wu_ref, wd_ref, g_post):
    h = x.astype(BF16)
    scale = _rms_scale(x)
    y = None
    off = 0
    for ch in FFN_CHUNKS:
        g = _dot(h, wg_ref[:, off:off + ch]) * scale
        u = _dot(h, wu_ref[:, off:off + ch]) * scale
        a = (g * jax.nn.sigmoid(g) * u).astype(BF16)
        part = _dot(a, wd_ref[off:off + ch, :])
        y = part if y is None else y + part
        off += ch
    return x + _rms(y, g_post)


def _ffn_kernel(x_ref, wg_ref, wu_ref, wd_ref, gpost_ref, o_ref, *, layer):
    g_post = _row(gpost_ref, layer)
    rows = x_ref.shape[0] // FFN_SUB
    for sub in range(FFN_SUB):
        r = slice(sub * rows, (sub + 1) * rows)
        o_ref[r, :] = _ffn_rows(x_ref[r, :], wg_ref, wu_ref, wd_ref, g_post)


def _ffn(x2, wg, wu, wd, gpost, layer):
    n = x2.shape[0]
    t = FFN_TILE
    row = pl.BlockSpec((t, D), lambda i: (i, 0))
    return pl.pallas_call(
        functools.partial(_ffn_kernel, layer=layer),
        grid=(n // t,),
        in_specs=[row, _const_spec((D, FFN_H)), _const_spec((D, FFN_H)), _const_spec((FFN_H, D)),
                  _const_spec(gpost.shape)],
        out_specs=row,
        out_shape=jax.ShapeDtypeStruct((n, D), F32),
        compiler_params=_params(1),
        name="swiglu",
    )(x2, wg, wu, wd, gpost)


def kernel(x, mem, ln_mix_pre, w_in, pool_w, pool_scale, sgu_norm, sgu_w, sgu_b, fnet_w, w_out,
           ln_mix_post, ln_xa_pre, ln_mem, xa_wq, xa_wk, xa_wv, xa_wo, ln_xa_post, ln_ffn_pre,
           ffn_wg, ffn_wu, ffn_wd, ln_ffn_post):
    batch, seq, d = x.shape
    depth = w_in.shape[0]
    assert d == D and sum(FFN_CHUNKS) == FFN_H
    assert all(seq % t == 0 for t in (IN_TILE, MIX_TILE, FFN_TILE))
    assert (MIX_TILE // MIX_SUB) % SGU_CHUNK == 0
    ka, fb, dftc, swap = (jnp.asarray(t).astype(BF16) for t in _fft_tables(seq))
    edge = jnp.asarray(_pool_edge_scales(seq))
    n = batch * seq
    x2 = x.reshape(n, d)
    bf = lambda a: a.astype(BF16)
    col = lambda g: g[:, :, None]

    sw_b = bf(sgu_w)
    sb_full = jnp.broadcast_to(sgu_b[:, :, :, None], sgu_b.shape + (SGU_CHUNK,))
    wcat = _fold(dftc, bf(_block_diag(fnet_w)), _block_diag(pool_w), pool_scale, w_out)
    kt_all, v_all = _kv_proj(mem, ln_mem, xa_wk, xa_wv)

    for l in range(depth):
        za, zu, zv, zc, wg_b, wu_b, wd_b, wq_b, wo_b = _in_proj(
            x2, col(ln_mix_pre), w_in, sgu_norm, ffn_wg, ffn_wu, ffn_wd, col(ln_ffn_pre),
            xa_wq, xa_wo, col(ln_xa_pre), l)
        a5 = _fft_stage_a(zc.reshape(batch, FFT_N1, FFT_N2, FNET_W), ka)
        fr, fi = _fft_stage_b(a5, fb, swap)
        x2 = _mix_attn(x2, za, zu, zv, fr.reshape(n, FNET_W), fi.reshape(n, FNET_W), sw_b, sb_full, wcat,
                       ln_mix_post, edge, kt_all, v_all, wq_b, wo_b, ln_xa_post, l, batch=batch, seq=seq)
        x2 = _ffn(x2, wg_b, wu_b, wd_b, ln_ffn_post, l)
    return x2.reshape(batch, seq, d)
```
